```python
import jax
import jax.numpy as jnp
from jax import lax
import numpy as np


D_MODEL = 1024
BATCH = 32
SEQ = 2048
DEPTH = 2

MIX_WIDTH = D_MODEL
N_MIXERS = 4
GROUP_WIDTH = MIX_WIDTH // N_MIXERS
NORM_EPS = 1e-6
NEG_INF = -1e30

MOBA_HEADS = 4
MOBA_HEAD_DIM = GROUP_WIDTH // MOBA_HEADS
MOBA_BLOCK = 256
MOBA_TOPK = 3
MOBA_Q_CHUNK = 64
ROPE_THETA = 10000.0

SSM_HEADS = 4
SSM_D_INNER = GROUP_WIDTH
SSM_HEAD_DIM = SSM_D_INNER // SSM_HEADS
SSM_GROUPS = 2
SSM_STATE = 64
SSM_CONV = 4
SSM_CHUNK = 128
SSM_CONV_CH = SSM_D_INNER + 2 * SSM_GROUPS * SSM_STATE

GLA_HEADS = 4
GLA_DV = GROUP_WIDTH // GLA_HEADS
GLA_DK = GLA_DV // 2
GLA_GATE_RANK = 16
GLA_GATE_TAU = 16.0
GLA_CHUNK = 64

RET_HEADS = 4
RET_DV = GROUP_WIDTH // RET_HEADS
RET_DK = RET_DV // 2
RET_CHUNK = 128

FFN_DIM = 2816
FFN_CONV = 3

IN_SIZES = (GROUP_WIDTH, GROUP_WIDTH, GROUP_WIDTH,
            SSM_D_INNER, SSM_CONV_CH, SSM_HEADS,
            GLA_HEADS * GLA_DK, GLA_HEADS * GLA_DK, GLA_HEADS * GLA_DV, GLA_HEADS * GLA_DV, GLA_GATE_RANK,
            RET_HEADS * RET_DK, RET_HEADS * RET_DK, RET_HEADS * RET_DV, RET_HEADS * RET_DV)
IN_WIDTH = sum(IN_SIZES)

kernel_name = 'hybrid_moba_ssd_gla_retention_convglu'


def rms_norm(x, g):
    xf = x.astype(jnp.float32)
    y = xf * lax.rsqrt(jnp.mean(xf * xf, axis=-1, keepdims=True) + NORM_EPS)
    return (y * g.astype(jnp.float32)).astype(x.dtype)


def rope_inv_freq(dim):
    return ROPE_THETA ** (-jnp.arange(0, dim, 2, dtype=jnp.float32) / dim)


def retnet_inv_freq(dim):
    return 1.0 / (ROPE_THETA ** jnp.linspace(0.0, 1.0, dim // 2, dtype=jnp.float32))


def apply_rotary(x, inv_freq):
    s = x.shape[1]
    ang = jnp.arange(s, dtype=jnp.float32)[:, None] * inv_freq[None, :]
    cos = jnp.cos(ang)[None, :, None, :]
    sin = jnp.sin(ang)[None, :, None, :]
    x1, x2 = jnp.split(x.astype(jnp.float32), 2, axis=-1)
    return jnp.concatenate([x1 * cos - x2 * sin, x2 * cos + x1 * sin], axis=-1).astype(x.dtype)


def causal_dwconv(x, w, b):
    k, c = w.shape
    y = lax.conv_general_dilated(x, w[:, None, :].astype(x.dtype), window_strides=(1,),
                                 padding=[(k - 1, 0)], dimension_numbers=('NWC', 'WIO', 'NWC'),
                                 feature_group_count=c)
    return y + b.astype(x.dtype)


def moba_attention(q, k, v, q_norm_g, k_norm_g):
    bsz, s, h, dh = q.shape
    inv = rope_inv_freq(dh)
    q = apply_rotary(rms_norm(q, q_norm_g), inv).transpose(0, 2, 1, 3)
    k = apply_rotary(rms_norm(k, k_norm_g), inv).transpose(0, 2, 1, 3)
    v = v.transpose(0, 2, 1, 3)
    n_blk = -(-s // MOBA_BLOCK)
    pad = n_blk * MOBA_BLOCK - s
    kb = jnp.pad(k, ((0, 0), (0, 0), (0, pad), (0, 0))).reshape(bsz, h, n_blk, MOBA_BLOCK, dh)
    vb = jnp.pad(v, ((0, 0), (0, 0), (0, pad), (0, 0))).reshape(bsz, h, n_blk, MOBA_BLOCK, dh)
    pos = jnp.arange(s)
    q_blk = pos // MOBA_BLOCK
    k_mean = jnp.mean(kb.astype(jnp.float32), axis=3)
    gate = jnp.einsum('bhsd,bhnd->bhsn', q.astype(jnp.float32), k_mean)
    fully_past = jnp.arange(n_blk)[None, :] < q_blk[:, None]
    gate = jnp.where(fully_past, gate, -jnp.inf)
    n_sel = min(MOBA_TOPK, n_blk)
    _, sel = lax.top_k(gate, n_sel)
    sel_valid = sel < q_blk[:, None]

    n_qc = s // MOBA_Q_CHUNK

    def to_chunks(t):
        t = t.reshape((bsz, h, n_qc, MOBA_Q_CHUNK) + t.shape[3:])
        return jnp.moveaxis(t, 2, 0)

    b_idx = jnp.arange(bsz)[:, None, None, None]
    h_idx = jnp.arange(h)[None, :, None, None]
    scale = dh ** -0.5

    def attend(args):
        q_c, sel_c, valid_c, c = args
        start = c * MOBA_Q_CHUNK
        own = start // MOBA_BLOCK
        q_pos = start + jnp.arange(MOBA_Q_CHUNK)
        k_pos = own * MOBA_BLOCK + jnp.arange(MOBA_BLOCK)
        k_sel = kb[b_idx, h_idx, sel_c].astype(jnp.float32)
        v_sel = vb[b_idx, h_idx, sel_c].astype(jnp.float32)
        k_own = lax.dynamic_index_in_dim(kb, own, axis=2, keepdims=False).astype(jnp.float32)
        v_own = lax.dynamic_index_in_dim(vb, own, axis=2, keepdims=False).astype(jnp.float32)
        qf = q_c.astype(jnp.float32)
        s_sel = jnp.einsum('bhqd,bhqnjd->bhqnj', qf, k_sel) * scale
        s_sel = jnp.where(valid_c[..., None], s_sel, NEG_INF)
        s_own = jnp.einsum('bhqd,bhjd->bhqj', qf, k_own) * scale
        s_own = jnp.where(k_pos[None, :] <= q_pos[:, None], s_own, NEG_INF)
        n_far = n_sel * MOBA_BLOCK
        probs = jax.nn.softmax(jnp.concatenate([s_sel.reshape(bsz, h, MOBA_Q_CHUNK, n_far), s_own], axis=-1), axis=-1)
        p_sel = probs[..., :n_far].reshape(bsz, h, MOBA_Q_CHUNK, n_sel, MOBA_BLOCK)
        p_own = probs[..., n_far:]
        o = jnp.einsum('bhqnj,bhqnjd->bhqd', p_sel, v_sel) + jnp.einsum('bhqj,bhjd->bhqd', p_own, v_own)
        return o.astype(v.dtype)

    out = lax.map(attend, (to_chunks(q), to_chunks(sel), to_chunks(sel_valid), jnp.arange(n_qc)))
    return out.transpose(1, 0, 3, 2, 4).reshape(bsz, s, h * dh)


def ssd_chunked(x, a, b, c):
    bsz, s, h, p = x.shape
    n = b.shape[-1]
    L = SSM_CHUNK
    nc = s // L
    x = x.reshape(bsz, nc, L, h, p)
    b = b.reshape(bsz, nc, L, h, n)
    c = c.reshape(bsz, nc, L, h, n)
    a = a.reshape(bsz, nc, L, h).transpose(0, 3, 1, 2)
    a_cs = jnp.cumsum(a, axis=-1)
    causal = jnp.tril(jnp.ones((L, L), dtype=bool))
    seg = a_cs[..., :, None] - a_cs[..., None, :]
    decay = jnp.exp(jnp.where(causal, seg, -jnp.inf))
    scores = jnp.einsum('bclhn,bcshn->bhcls', c, b) * decay
    y_diag = jnp.einsum('bhcls,bcshp->bclhp', scores, x)
    decay_to_end = jnp.exp(a_cs[..., -1:] - a_cs)
    states = jnp.einsum('bclhn,bhcl,bclhp->bchpn', b, decay_to_end, x)
    chunk_decay = jnp.exp(a_cs[..., -1])

    def step(hs, inp):
        st, dec = inp
        return hs * dec[..., None, None] + st, hs

    h0 = jnp.zeros((bsz, h, p, n), jnp.float32)
    _, prev = lax.scan(step, h0, (jnp.moveaxis(states, 1, 0), jnp.moveaxis(chunk_decay, 2, 0)))
    prev = jnp.moveaxis(prev, 0, 1)
    y_off = jnp.einsum('bclhn,bchpn,bhcl->bclhp', c, prev, jnp.exp(a_cs))
    return (y_diag + y_off).reshape(bsz, s, h, p)


def mamba2_mixer(z, xbc, dt_raw, conv_w, conv_b, dt_bias, a_log, d_skip, norm_g):
    bsz, s, _ = z.shape
    f32 = jnp.float32
    xbc = jax.nn.silu(causal_dwconv(xbc, conv_w, conv_b)).astype(f32)
    xs, bm, cm = jnp.split(xbc, [SSM_D_INNER, SSM_D_INNER + SSM_GROUPS * SSM_STATE], axis=-1)
    xs = xs.reshape(bsz, s, SSM_HEADS, SSM_HEAD_DIM)
    rep = SSM_HEADS // SSM_GROUPS
    bm = jnp.repeat(bm.reshape(bsz, s, SSM_GROUPS, SSM_STATE), rep, axis=2)
    cm = jnp.repeat(cm.reshape(bsz, s, SSM_GROUPS, SSM_STATE), rep, axis=2)
    dt = jax.nn.softplus(dt_raw.astype(f32) + dt_bias.astype(f32))
    a = -jnp.exp(a_log.astype(f32))
    y = ssd_chunked(xs * dt[..., None], dt * a, bm, cm)
    y = y + xs * d_skip.astype(f32)[:, None]
    y = y.reshape(bsz, s, SSM_D_INNER) * jax.nn.silu(z.astype(f32))
    y = rms_norm(y.reshape(bsz, s, SSM_GROUPS, SSM_D_INNER // SSM_GROUPS),
                 norm_g.reshape(SSM_GROUPS, SSM_D_INNER // SSM_GROUPS))
    return y.reshape(bsz, s, SSM_D_INNER).astype(z.dtype)


def gla_chunked(q, k, v, g):
    bsz, s, h, dk = q.shape
    dv = v.shape[-1]
    L = GLA_CHUNK
    nc = s // L
    q = q.reshape(bsz, nc, L, h, dk)
    k = k.reshape(bsz, nc, L, h, dk)
    v = v.reshape(bsz, nc, L, h, dv)
    bcs = jnp.cumsum(g.reshape(bsz, nc, L, h, dk), axis=2)
    q_dec = q * jnp.exp(bcs)
    k_inv = k * jnp.exp(-bcs)
    causal = jnp.tril(jnp.ones((L, L), dtype=bool))
    att = jnp.where(causal, jnp.einsum('bclhd,bcshd->bchls', q_dec, k_inv), 0.0)
    o_intra = jnp.einsum('bchls,bcshe->bclhe', att, v)
    b_last = bcs[:, :, -1]
    k_end = k * jnp.exp(b_last[:, :, None] - bcs)
    states = jnp.einsum('bclhd,bclhe->bchde', k_end, v)

    def step(st, inp):
        new, dec = inp
        return st * dec[..., None] + new, st

    s0 = jnp.zeros((bsz, h, dk, dv), jnp.float32)
    _, prev = lax.scan(step, s0, (jnp.moveaxis(states, 1, 0), jnp.moveaxis(jnp.exp(b_last), 1, 0)))
    prev = jnp.moveaxis(prev, 0, 1)
    o_inter = jnp.einsum('bclhd,bchde->bclhe', q_dec, prev)
    return (o_intra + o_inter).reshape(bsz, s, h, dv)


def gla_mixer(q, k, v, r, g_lr, w2, b2, norm_g):
    bsz, s, _ = q.shape
    f32 = jnp.float32
    q = q.astype(f32).reshape(bsz, s, GLA_HEADS, GLA_DK) * (GLA_DK ** -0.5)
    k = k.astype(f32).reshape(bsz, s, GLA_HEADS, GLA_DK)
    v = v.astype(f32).reshape(bsz, s, GLA_HEADS, GLA_DV)
    g_pre = jnp.einsum('bsr,rk->bsk', g_lr.astype(f32), w2.astype(f32)) + b2.astype(f32)
    g = (jax.nn.log_sigmoid(g_pre) / GLA_GATE_TAU).reshape(bsz, s, GLA_HEADS, GLA_DK)
    o = rms_norm(gla_chunked(q, k, v, g), norm_g).reshape(bsz, s, GLA_HEADS * GLA_DV)
    return (o * jax.nn.silu(r.astype(f32))).astype(r.dtype)


def retention_chunked(q, k, v, lg):
    bsz, s, h, dk = q.shape
    dv = v.shape[-1]
    L = RET_CHUNK
    nc = s // L
    q = q.reshape(bsz, nc, L, h, dk)
    k = k.reshape(bsz, nc, L, h, dk)
    v = v.reshape(bsz, nc, L, h, dv)
    idx = jnp.arange(L, dtype=jnp.float32)
    diff = idx[:, None] - idx[None, :]
    decay_mat = jnp.where(diff >= 0, jnp.exp(jnp.maximum(diff, 0.0)[None] * lg[:, None, None]), 0.0)
    att = jnp.einsum('bclhd,bcshd->bchls', q, k) * decay_mat[None, None]
    o_intra = jnp.einsum('bchls,bcshe->bclhe', att, v)
    k_end = k * jnp.exp((L - 1 - idx)[:, None] * lg[None, :])[None, None, :, :, None]
    states = jnp.einsum('bclhd,bclhe->bchde', k_end, v)
    chunk_dec = jnp.exp(L * lg)

    def step(st, new):
        return st * chunk_dec[None, :, None, None] + new, st

    s0 = jnp.zeros((bsz, h, dk, dv), jnp.float32)
    _, prev = lax.scan(step, s0, jnp.moveaxis(states, 1, 0))
    prev = jnp.moveaxis(prev, 0, 1)
    q_dec = q * jnp.exp((idx + 1.0)[:, None] * lg[None, :])[None, None, :, :, None]
    o_inter = jnp.einsum('bclhd,bchde->bclhe', q_dec, prev)
    return (o_intra + o_inter).reshape(bsz, s, h, dv)


def retention_mixer(q, k, v, g, norm_g):
    bsz, s, _ = q.shape
    f32 = jnp.float32
    inv = retnet_inv_freq(RET_DK)
    q = apply_rotary(q.astype(f32).reshape(bsz, s, RET_HEADS, RET_DK), inv)
    k = apply_rotary(k.astype(f32).reshape(bsz, s, RET_HEADS, RET_DK), inv) * (RET_DK ** -0.5)
    v = v.astype(f32).reshape(bsz, s, RET_HEADS, RET_DV)
    lg = jnp.log(1.0 - 2.0 ** (-5.0 - jnp.arange(RET_HEADS, dtype=f32)))
    o = rms_norm(retention_chunked(q, k, v, lg), norm_g).reshape(bsz, s, RET_HEADS * RET_DV)
    return (o * jax.nn.silu(g.astype(f32))).astype(g.dtype)


def conv_glu_ffn(x, w_up, conv_w, conv_b, w_down):
    u = jnp.einsum('bsd,df->bsf', x, w_up)
    u = causal_dwconv(u, conv_w, conv_b)
    gate, val = jnp.split(u, 2, axis=-1)
    return jnp.einsum('bsf,fd->bsd', jax.nn.silu(gate) * val, w_down)


def setup_inputs(seed: int = 0) -> dict:
    key = jax.random.key(seed)
    ks = jax.random.split(key, 24)
    f32 = jnp.float32

    def nrm(k, shape, scale):
        return scale * jax.random.normal(k, shape, f32)

    def gain(k, n):
        return 1.0 + 0.01 * jax.random.normal(k, (DEPTH, n), f32)

    dt = jnp.exp(jax.random.uniform(ks[7], (DEPTH, SSM_HEADS), f32, float(np.log(1e-3)), float(np.log(1e-1))))
    return {
        'x': jax.random.normal(ks[0], (BATCH, SEQ, D_MODEL), f32),
        'attn_norm_g': gain(ks[1], D_MODEL),
        'w_in': nrm(ks[2], (DEPTH, D_MODEL, IN_WIDTH), D_MODEL ** -0.5),
        'moba_q_norm_g': gain(ks[3], MOBA_HEAD_DIM),
        'moba_k_norm_g': gain(ks[4], MOBA_HEAD_DIM),
        'ssm_conv_w': nrm(ks[5], (DEPTH, SSM_CONV, SSM_CONV_CH), SSM_CONV ** -0.5),
        'ssm_conv_b': nrm(ks[6], (DEPTH, SSM_CONV_CH), 0.01),
        'ssm_dt_bias': dt + jnp.log(-jnp.expm1(-dt)),
        'ssm_a_log': jnp.log(jax.random.uniform(ks[8], (DEPTH, SSM_HEADS), f32, 1.0, 16.0)),
        'ssm_d': gain(ks[9], SSM_HEADS),
        'ssm_norm_g': gain(ks[10], SSM_D_INNER),
        'gla_gate_w2': nrm(ks[11], (DEPTH, GLA_GATE_RANK, GLA_HEADS * GLA_DK), GLA_GATE_RANK ** -0.5),
        'gla_gate_b': nrm(ks[12], (DEPTH, GLA_HEADS * GLA_DK), 0.01),
        'gla_norm_g': gain(ks[13], GLA_DV),
        'ret_norm_g': gain(ks[14], RET_DV),
        'w_out': nrm(ks[15], (DEPTH, MIX_WIDTH, D_MODEL), MIX_WIDTH ** -0.5),
        'ffn_norm_g': gain(ks[16], D_MODEL),
        'ffn_w_up': nrm(ks[17], (DEPTH, D_MODEL, 2 * FFN_DIM), D_MODEL ** -0.5),
        'ffn_conv_w': nrm(ks[18], (DEPTH, FFN_CONV, 2 * FFN_DIM), FFN_CONV ** -0.5),
        'ffn_conv_b': nrm(ks[19], (DEPTH, 2 * FFN_DIM), 0.01),
        'ffn_w_down': nrm(ks[20], (DEPTH, FFN_DIM, D_MODEL), FFN_DIM ** -0.5),
    }


def reference(x, attn_norm_g, w_in, moba_q_norm_g, moba_k_norm_g, ssm_conv_w, ssm_conv_b,
              ssm_dt_bias, ssm_a_log, ssm_d, ssm_norm_g, gla_gate_w2, gla_gate_b, gla_norm_g,
              ret_norm_g, w_out, ffn_norm_g, ffn_w_up, ffn_conv_w, ffn_conv_b, ffn_w_down):
    bsz, s, _ = x.shape
    splits = [int(v) for v in np.cumsum(IN_SIZES)[:-1]]
    for l in range(DEPTH):
        h = rms_norm(x, attn_norm_g[l])
        proj = jnp.einsum('bsd,de->bse', h, w_in[l])
        (mq, mk, mv, sz, sxbc, sdt, gq, gk, gv, gr, gg, rq, rk, rv, rg) = jnp.split(proj, splits, axis=-1)
        y_moba = moba_attention(mq.reshape(bsz, s, MOBA_HEADS, MOBA_HEAD_DIM),
                                mk.reshape(bsz, s, MOBA_HEADS, MOBA_HEAD_DIM),
                                mv.reshape(bsz, s, MOBA_HEADS, MOBA_HEAD_DIM),
                                moba_q_norm_g[l], moba_k_norm_g[l])
        y_ssm = mamba2_mixer(sz, sxbc, sdt, ssm_conv_w[l], ssm_conv_b[l], ssm_dt_bias[l],
                             ssm_a_log[l], ssm_d[l], ssm_norm_g[l])
        y_gla = gla_mixer(gq, gk, gv, gr, gg, gla_gate_w2[l], gla_gate_b[l], gla_norm_g[l])
        y_ret = retention_mixer(rq, rk, rv, rg, ret_norm_g[l])
        mixed = jnp.concatenate([y_moba, y_ssm, y_gla, y_ret], axis=-1)
        x = x + jnp.einsum('bse,ed->bsd', mixed, w_out[l])
        x = x + conv_glu_ffn(rms_norm(x, ffn_norm_g[l]), ffn_w_up[l], ffn_conv_w[l],
                             ffn_conv_b[l], ffn_w_down[l])
    return x
```

```python
import functools

import numpy as np
import jax
import jax.numpy as jnp
from jax import lax
from jax.experimental import pallas as pl
from jax.experimental.pallas import tpu as pltpu

F32 = jnp.float32
BF16 = jnp.bfloat16
HIGHEST = lax.Precision.HIGHEST

D_MODEL = 1024
GROUP_WIDTH = 256
NORM_EPS = 1e-6
NEG_INF = -1e30

MOBA_HEADS = 4
MOBA_HEAD_DIM = 64
MOBA_BLOCK = 256
MOBA_TOPK = 3
ROPE_THETA = 10000.0

SSM_HEADS = 4
SSM_HEAD_DIM = 64
SSM_GROUPS = 2
SSM_STATE = 64
SSM_CONV = 4
SSM_CHUNK = 128
SSM_CONV_CH = 512

GLA_HEADS = 4
GLA_DV = 64
GLA_DK = 32
GLA_GATE_RANK = 16
GLA_GATE_TAU = 16.0
GLA_CHUNK = 64

RET_HEADS = 4
RET_DV = 64
RET_DK = 32
RET_CHUNK = 128

FFN_DIM = 2816
FFN_CONV = 3

LANES = 128
SUBLANES = 8
VMEM_LIMIT_BYTES = 48 * 1024 * 1024

MIX_TILE = 512
FFN_TILE = 512
FFN_COLS = 256


def _rms_rows(x, g):
    ms = jnp.mean(x * x, axis=-1, keepdims=True)
    return x * lax.rsqrt(ms + NORM_EPS) * g


def _dot(a, b):
    return jnp.dot(a.astype(BF16), b.astype(BF16), preferred_element_type=F32)


def _dot_nt(a, b):
    return lax.dot_general(a.astype(BF16), b.astype(BF16), (((1,), (1,)), ((), ())),
                           preferred_element_type=F32)


def _dot_f32(a, b):
    return jnp.dot(a, b, precision=HIGHEST, preferred_element_type=F32)


def _sigmoid(x):
    return 1.0 / (1.0 + jnp.exp(-x))


def _silu(x):
    return x * _sigmoid(x)


def _softplus(x):
    return jnp.maximum(x, 0.0) + jnp.log1p(jnp.exp(-jnp.abs(x)))


def _rotate_half(t, half):
    width = t.shape[-1]
    lane = lax.broadcasted_iota(jnp.int32, t.shape, t.ndim - 1)
    first = (lane % (2 * half)) < half
    return jnp.where(first, pltpu.roll(t, width - half, t.ndim - 1), pltpu.roll(t, half, t.ndim - 1))


def _moba_kernel(x_ref, g_ref, w_ref, qg_ref, kg_ref, cos_ref, sin_ref, gm_ref, hmask_ref, y_ref,
                 k_s, vt_s, km_s, qm_s, bias_s, m_s, l_s, acc_s, *, n_blk):
    i = pl.program_id(1)
    blk = MOBA_BLOCK
    gw = GROUP_WIDTH
    dh = MOBA_HEAD_DIM

    h = _rms_rows(x_ref[...], g_ref[...]).astype(BF16)
    p = jnp.dot(h, w_ref[...], preferred_element_type=F32)
    gm = gm_ref[...]
    cos = cos_ref[...]
    sin = sin_ref[...]

    def prep(t, gain):
        ms = _dot_f32(t * t, gm)
        t = t * lax.rsqrt(ms + NORM_EPS) * gain
        return t * cos + _rotate_half(t, dh // 2) * sin

    q = prep(p[:, 0:gw], qg_ref[...])
    k = prep(p[:, gw:2 * gw], kg_ref[...])
    v = p[:, 2 * gw:3 * gw]

    @pl.when(i == 0)
    def _():
        km_s[...] = jnp.zeros_like(km_s)

    kb = k.astype(BF16)
    vt = v.T.astype(BF16)
    k_s[i] = kb
    vt_s[i] = vt
    km_s[i] = jnp.broadcast_to(jnp.mean(k, axis=0, keepdims=True), (SUBLANES, gw))

    row8 = lax.broadcasted_iota(jnp.int32, (SUBLANES, gw), 0)
    kmat = jnp.zeros((SUBLANES, gw), F32)
    for n in range(n_blk):
        kmat = jnp.where(row8 == n, km_s[n], kmat)
    hmask = hmask_ref[...]
    kmh = jnp.concatenate([kmat * hmask[hd:hd + 1, :] for hd in range(MOBA_HEADS)], axis=0)
    gate = lax.dot_general(kmh, q, (((1,), (1,)), ((), ())), precision=HIGHEST,
                           preferred_element_type=F32)

    rowq = lax.broadcasted_iota(jnp.int32, (SUBLANES, blk), 0)
    valid = rowq < i
    bias_tiles = [jnp.zeros((SUBLANES, blk), F32) for _ in range(n_blk)]
    for hd in range(MOBA_HEADS):
        gh = gate[hd * SUBLANES:(hd + 1) * SUBLANES, :]
        for n in range(n_blk):
            gn = gh[n:n + 1, :]
            beats = valid & ((gh > gn) | ((gh == gn) & (rowq < n)))
            cnt = jnp.sum(beats.astype(F32), axis=0, keepdims=True)
            bias_n = jnp.where(cnt < float(MOBA_TOPK), 0.0, NEG_INF)
            bias_tiles[n] = jnp.where(rowq == hd, bias_n, bias_tiles[n])
    for n in range(n_blk):
        bias_s[n] = bias_tiles[n]

    rj = lax.broadcasted_iota(jnp.int32, (blk, blk), 0)
    cq = lax.broadcasted_iota(jnp.int32, (blk, blk), 1)
    causal = rj <= cq
    scale = dh ** -0.5
    for hd in range(MOBA_HEADS):
        qh = (q * (scale * hmask[hd:hd + 1, :])).astype(BF16)
        qm_s[hd] = qh
        s = jnp.where(causal, _dot_nt(kb, qh), NEG_INF)
        m = jnp.max(s, axis=0, keepdims=True)
        pr = jnp.exp(s - m)
        m_s[hd:hd + 1, :] = m
        l_s[hd:hd + 1, :] = jnp.sum(pr, axis=0, keepdims=True)
        acc_s[hd * dh:(hd + 1) * dh, :] = jnp.dot(vt[hd * dh:(hd + 1) * dh, :], pr.astype(BF16),
                                                  preferred_element_type=F32)

    def past_block(n, carry):
        kn = k_s[n]
        vtn = vt_s[n]
        bt = bias_s[n]
        for hd in range(MOBA_HEADS):
            s = _dot_nt(kn, qm_s[hd]) + bt[hd:hd + 1, :]
            m_old = m_s[hd:hd + 1, :]
            m_new = jnp.maximum(m_old, jnp.max(s, axis=0, keepdims=True))
            alpha = jnp.exp(m_old - m_new)
            pr = jnp.exp(s - m_new)
            l_s[hd:hd + 1, :] = alpha * l_s[hd:hd + 1, :] + jnp.sum(pr, axis=0, keepdims=True)
            acc_s[hd * dh:(hd + 1) * dh, :] = alpha * acc_s[hd * dh:(hd + 1) * dh, :] + jnp.dot(
                vtn[hd * dh:(hd + 1) * dh, :], pr.astype(BF16), preferred_element_type=F32)
            m_s[hd:hd + 1, :] = m_new
        return carry

    lax.fori_loop(0, i, past_block, 0)

    o_t = jnp.concatenate(
        [acc_s[hd * dh:(hd + 1) * dh, :] * (1.0 / l_s[hd:hd + 1, :]) for hd in range(MOBA_HEADS)], axis=0)
    y_ref[...] = o_t.T.astype(y_ref.dtype)


def _moba_call(x, norm_g, w, qg, kg, cos_t, sin_t, gm, hmask):
    bsz, s, d = x.shape
    blk = MOBA_BLOCK
    n_blk = s // blk
    gw = GROUP_WIDTH
    const = lambda b, i: (0, 0)
    return pl.pallas_call(
        functools.partial(_moba_kernel, n_blk=n_blk),
        grid=(bsz, n_blk),
        in_specs=[
            pl.BlockSpec((None, blk, d), lambda b, i: (b, i, 0)),
            pl.BlockSpec((1, d), const),
            pl.BlockSpec((d, 3 * gw), const),
            pl.BlockSpec((1, gw), const),
            pl.BlockSpec((1, gw), const),
            pl.BlockSpec((blk, gw), lambda b, i: (i, 0)),
            pl.BlockSpec((blk, gw), lambda b, i: (i, 0)),
            pl.BlockSpec((gw, gw), const),
            pl.BlockSpec((SUBLANES, gw), const),
        ],
        out_specs=pl.BlockSpec((None, blk, gw), lambda b, i: (b, i, 0)),
        out_shape=jax.ShapeDtypeStruct((bsz, s, gw), BF16),
        scratch_shapes=[
            pltpu.VMEM((n_blk, blk, gw), BF16),
            pltpu.VMEM((n_blk, gw, blk), BF16),
            pltpu.VMEM((n_blk, SUBLANES, gw), F32),
            pltpu.VMEM((MOBA_HEADS, blk, gw), BF16),
            pltpu.VMEM((n_blk, SUBLANES, blk), F32),
            pltpu.VMEM((SUBLANES, blk), F32),
            pltpu.VMEM((SUBLANES, blk), F32),
            pltpu.VMEM((gw, blk), F32),
        ],
        compiler_params=pltpu.CompilerParams(
            dimension_semantics=("parallel", "arbitrary"), vmem_limit_bytes=VMEM_LIMIT_BYTES),
    )(x, norm_g, w, qg, kg, cos_t, sin_t, gm, hmask)


def _ssd_kernel(x_ref, g_ref, w_ref, cw_ref, cb_ref, dtb_ref, alog_ref, dsk_ref, ng_ref, tri_ref,
                gmask_ref, hmask_ref, smask_ref, y_ref, xp_s, st_s, y_s, *, tile):
    t = pl.program_id(1)
    gw = GROUP_WIDTH
    L = SSM_CHUNK
    gn = SSM_GROUPS * SSM_STATE

    @pl.when(t == 0)
    def _():
        xp_s[0:SUBLANES, :] = jnp.zeros((SUBLANES, SSM_CONV_CH), F32)
        st_s[...] = jnp.zeros_like(st_s)

    h = _rms_rows(x_ref[...], g_ref[...]).astype(BF16)
    p = jnp.dot(h, w_ref[...], preferred_element_type=F32)
    z = p[:, 0:gw]
    xp_s[SUBLANES:SUBLANES + tile, :] = p[:, gw:gw + SSM_CONV_CH]
    cw = cw_ref[...]
    conv = cb_ref[...]
    for j in range(SSM_CONV):
        conv = conv + cw[j:j + 1, :] * xp_s[pl.ds(SUBLANES - (SSM_CONV - 1) + j, tile), :]
    xp_s[0:SUBLANES, :] = xp_s[tile:tile + SUBLANES, :]
    xbc = _silu(conv)
    xs = xbc[:, 0:gw]
    bm = xbc[:, gw:gw + gn]
    cm = xbc[:, gw + gn:gw + 2 * gn]
    dt = _softplus(p[:, gw + SSM_CONV_CH:2 * gw + SSM_CONV_CH] + dtb_ref[...])
    a = dt * (-jnp.exp(alog_ref[...]))
    xdt = xs * dt

    tri = tri_ref[...]
    gmask = gmask_ref[...]
    hmask = hmask_ref[...]
    smask = smask_ref[...]
    rl = lax.broadcasted_iota(jnp.int32, (L, L), 0)
    cs = lax.broadcasted_iota(jnp.int32, (L, L), 1)
    causal = rl >= cs
    rep = SSM_HEADS // SSM_GROUPS
    for c in range(tile // L):
        sl = slice(c * L, (c + 1) * L)
        a_cs = _dot_f32(tri, a[sl])
        a_cs_t = a_cs.T
        a_last = a_cs[L - 1:L, :]
        cmc = cm[sl]
        bmc = bm[sl]
        xdtc = xdt[sl]
        y = jnp.zeros((L, gw), F32)
        for g in range(SSM_GROUPS):
            scores = _dot_nt(cmc * gmask[g:g + 1, :], bmc)
            for hd in range(g * rep, (g + 1) * rep):
                col = a_cs[:, hd * SSM_HEAD_DIM:hd * SSM_HEAD_DIM + 1]
                row = a_cs_t[hd * SSM_HEAD_DIM:hd * SSM_HEAD_DIM + 1, :]
                decay = jnp.where(causal, jnp.exp(col - row), 0.0)
                y = y + _dot(scores * decay, xdtc * hmask[hd:hd + 1, :])
        state = st_s[...]
        y = y + _dot(cmc, state) * jnp.exp(a_cs)
        new = _dot(bmc.T, xdtc * jnp.exp(a_last - a_cs)) * smask
        st_s[...] = state * jnp.exp(a_last) + new
        y_s[sl, :] = y

    y = (y_s[...] + xs * dsk_ref[...]) * _silu(z)
    ng = ng_ref[...]
    half = gw // SSM_GROUPS
    for g in range(SSM_GROUPS):
        yg = y[:, g * half:(g + 1) * half]
        ms = jnp.mean(yg * yg, axis=-1, keepdims=True)
        y_ref[:, g * half:(g + 1) * half] = (yg * lax.rsqrt(ms + NORM_EPS) * ng[:, g * half:(g + 1) * half]
                                             ).astype(y_ref.dtype)


def _ssd_call(x, norm_g, w, cw, cb, dtb, alog, dsk, ng, tri, gmask, hmask, smask):
    bsz, s, d = x.shape
    tile = min(MIX_TILE, s)
    gw = GROUP_WIDTH
    wcols = w.shape[1]
    gn = SSM_GROUPS * SSM_STATE
    const = lambda b, t: (0, 0)
    return pl.pallas_call(
        functools.partial(_ssd_kernel, tile=tile),
        grid=(bsz, s // tile),
        in_specs=[
            pl.BlockSpec((None, tile, d), lambda b, t: (b, t, 0)),
            pl.BlockSpec((1, d), const),
            pl.BlockSpec((d, wcols), const),
            pl.BlockSpec((SSM_CONV, SSM_CONV_CH), const),
            pl.BlockSpec((1, SSM_CONV_CH), const),
            pl.BlockSpec((1, gw), const),
            pl.BlockSpec((1, gw), const),
            pl.BlockSpec((1, gw), const),
            pl.BlockSpec((1, gw), const),
            pl.BlockSpec((SSM_CHUNK, SSM_CHUNK), const),
            pl.BlockSpec((SUBLANES, gn), const),
            pl.BlockSpec((SUBLANES, gw), const),
            pl.BlockSpec((gn, gw), const),
        ],
        out_specs=pl.BlockSpec((None, tile, gw), lambda b, t: (b, t, 0)),
        out_shape=jax.ShapeDtypeStruct((bsz, s, gw), BF16),
        scratch_shapes=[
            pltpu.VMEM((tile + SUBLANES, SSM_CONV_CH), F32),
            pltpu.VMEM((gn, gw), F32),
            pltpu.VMEM((tile, gw), F32),
        ],
        compiler_params=pltpu.CompilerParams(
            dimension_semantics=("parallel", "arbitrary"), vmem_limit_bytes=VMEM_LIMIT_BYTES),
    )(x, norm_g, w, cw, cb, dtb, alog, dsk, ng, tri, gmask, hmask, smask)


def _gla_kernel(x_ref, g_ref, w_ref, w2_ref, b2_ref, ng_ref, tri_ref, ones_ref, gm_ref, qmask_ref,
                vmask_ref, smask_ref, y_ref, st_s, o_s, *, tile):
    t = pl.program_id(1)
    gw = GROUP_WIDTH
    L = GLA_CHUNK
    P = 2 * L
    qk = GLA_HEADS * GLA_DK

    @pl.when(t == 0)
    def _():
        st_s[...] = jnp.zeros_like(st_s)

    h = _rms_rows(x_ref[...], g_ref[...]).astype(BF16)
    p = jnp.dot(h, w_ref[...], preferred_element_type=F32)
    q = p[:, 0:qk] * (GLA_DK ** -0.5)
    k = p[:, qk:2 * qk]
    v = p[:, 2 * qk:2 * qk + gw]
    r = p[:, 2 * qk + gw:2 * qk + 2 * gw]
    g_pre = _dot(p[:, 2 * qk + 2 * gw:2 * qk + 2 * gw + LANES], w2_ref[...]) + b2_ref[...]
    lg = (jnp.minimum(g_pre, 0.0) - jnp.log1p(jnp.exp(-jnp.abs(g_pre)))) / GLA_GATE_TAU

    tri = tri_ref[...]
    ones_bd = ones_ref[...]
    qmask = qmask_ref[...]
    vmask = vmask_ref[...]
    smask = smask_ref[...]
    rl = lax.broadcasted_iota(jnp.int32, (P, P), 0)
    cs = lax.broadcasted_iota(jnp.int32, (P, P), 1)
    causal = (rl >= cs) & ((rl // L) == (cs // L))
    lane_p = lax.broadcasted_iota(jnp.int32, (qk, P), 1)
    for c in range(tile // P):
        sl = slice(c * P, (c + 1) * P)
        gc = lg[sl]
        bcs = _dot_f32(tri, gc)
        b_last = _dot_f32(ones_bd, gc)
        qd = q[sl] * jnp.exp(bcs)
        ki = k[sl] * jnp.exp(-bcs)
        ke_t = (k[sl] * jnp.exp(b_last - bcs)).T
        dec_t = jnp.exp(b_last).T
        vc = v[sl]
        o = jnp.zeros((P, gw), F32)
        for hd in range(GLA_HEADS):
            att = jnp.where(causal, _dot_nt(qd * qmask[hd:hd + 1, :], ki), 0.0)
            o = o + _dot(att, vc * vmask[hd:hd + 1, :])
        o_s[sl, :] = o
        for j in range(2):
            rows = slice(c * P + j * L, c * P + (j + 1) * L)
            state = st_s[...]
            o_s[rows, :] = o_s[rows, :] + _dot(qd[j * L:(j + 1) * L], state)
            in_chunk = (lane_p // L) == j
            new = _dot(jnp.where(in_chunk, ke_t, 0.0), vc) * smask
            st_s[...] = state * dec_t[:, j * L:j * L + 1] + new

    o = o_s[...]
    ms = _dot_f32(o * o, gm_ref[...])
    y_ref[...] = (o * lax.rsqrt(ms + NORM_EPS) * ng_ref[...] * _silu(r)).astype(y_ref.dtype)


def _gla_call(x, norm_g, w, w2, b2, ng, tri, ones_bd, gm, qmask, vmask, smask):
    bsz, s, d = x.shape
    tile = min(MIX_TILE, s)
    gw = GROUP_WIDTH
    qk = GLA_HEADS * GLA_DK
    P = 2 * GLA_CHUNK
    const = lambda b, t: (0, 0)
    return pl.pallas_call(
        functools.partial(_gla_kernel, tile=tile),
        grid=(bsz, s // tile),
        in_specs=[
            pl.BlockSpec((None, tile, d), lambda b, t: (b, t, 0)),
            pl.BlockSpec((1, d), const),
            pl.BlockSpec((d, w.shape[1]), const),
            pl.BlockSpec((LANES, qk), const),
            pl.BlockSpec((1, qk), const),
            pl.BlockSpec((1, gw), const),
            pl.BlockSpec((P, P), const),
            pl.BlockSpec((P, P), const),
            pl.BlockSpec((gw, gw), const),
            pl.BlockSpec((SUBLANES, qk), const),
            pl.BlockSpec((SUBLANES, gw), const),
            pl.BlockSpec((qk, gw), const),
        ],
        out_specs=pl.BlockSpec((None, tile, gw), lambda b, t: (b, t, 0)),
        out_shape=jax.ShapeDtypeStruct((bsz, s, gw), BF16),
        scratch_shapes=[
            pltpu.VMEM((qk, gw), F32),
            pltpu.VMEM((tile, gw), F32),
        ],
        compiler_params=pltpu.CompilerParams(
            dimension_semantics=("parallel", "arbitrary"), vmem_limit_bytes=VMEM_LIMIT_BYTES),
    )(x, norm_g, w, w2, b2, ng, tri, ones_bd, gm, qmask, vmask, smask)


def _ret_kernel(x_ref, g_ref, w_ref, cos_ref, sin_ref, ng_ref, dmat_ref, kend_ref, qdec_ref, cdec_ref,
                gm_ref, qmask_ref, vmask_ref, smask_ref, y_ref, st_s, o_s, *, tile):
    t = pl.program_id(1)
    gw = GROUP_WIDTH
    L = RET_CHUNK
    qk = RET_HEADS * RET_DK

    @pl.when(t == 0)
    def _():
        st_s[...] = jnp.zeros_like(st_s)

    h = _rms_rows(x_ref[...], g_ref[...]).astype(BF16)
    p = jnp.dot(h, w_ref[...], preferred_element_type=F32)
    cos = cos_ref[...]
    sin = sin_ref[...]

    def rope(u):
        return u * cos + _rotate_half(u, RET_DK // 2) * sin

    q = rope(p[:, 0:qk])
    k = rope(p[:, qk:2 * qk]) * (RET_DK ** -0.5)
    v = p[:, 2 * qk:2 * qk + gw]
    gate = p[:, 2 * qk + gw:2 * qk + 2 * gw]

    kend = kend_ref[...]
    qdec = qdec_ref[...]
    cdec = cdec_ref[...]
    qmask = qmask_ref[...]
    vmask = vmask_ref[...]
    smask = smask_ref[...]
    for c in range(tile // L):
        sl = slice(c * L, (c + 1) * L)
        qc = q[sl]
        kc = k[sl]
        vc = v[sl]
        o = jnp.zeros((L, gw), F32)
        for hd in range(RET_HEADS):
            att = _dot_nt(qc * qmask[hd:hd + 1, :], kc) * dmat_ref[hd]
            o = o + _dot(att, vc * vmask[hd:hd + 1, :])
        state = st_s[...]
        o = o + _dot(qc * qdec, state)
        new = _dot((kc * kend).T, vc) * smask
        st_s[...] = state * cdec + new
        o_s[sl, :] = o

    o = o_s[...]
    ms = _dot_f32(o * o, gm_ref[...])
    y_ref[...] = (o * lax.rsqrt(ms + NORM_EPS) * ng_ref[...] * _silu(gate)).astype(y_ref.dtype)


def _ret_call(x, norm_g, w, cos_t, sin_t, ng, dmat, kend, qdec, cdec, gm, qmask, vmask, smask):
    bsz, s, d = x.shape
    tile = min(MIX_TILE, s)
    gw = GROUP_WIDTH
    qk = RET_HEADS * RET_DK
    L = RET_CHUNK
    const = lambda b, t: (0, 0)
    return pl.pallas_call(
        functools.partial(_ret_kernel, tile=tile),
        grid=(bsz, s // tile),
        in_specs=[
            pl.BlockSpec((None, tile, d), lambda b, t: (b, t, 0)),
            pl.BlockSpec((1, d), const),
            pl.BlockSpec((d, w.shape[1]), const),
            pl.BlockSpec((tile, qk), lambda b, t: (t, 0)),
            pl.BlockSpec((tile, qk), lambda b, t: (t, 0)),
            pl.BlockSpec((1, gw), const),
            pl.BlockSpec((RET_HEADS, L, L), lambda b, t: (0, 0, 0)),
            pl.BlockSpec((L, qk), const),
            pl.BlockSpec((L, qk), const),
            pl.BlockSpec((1, gw), const),
            pl.BlockSpec((gw, gw), const),
            pl.BlockSpec((SUBLANES, qk), const),
            pl.BlockSpec((SUBLANES, gw), const),
            pl.BlockSpec((qk, gw), const),
        ],
        out_specs=pl.BlockSpec((None, tile, gw), lambda b, t: (b, t, 0)),
        out_shape=jax.ShapeDtypeStruct((bsz, s, gw), BF16),
        scratch_shapes=[
            pltpu.VMEM((qk, gw), F32),
            pltpu.VMEM((tile, gw), F32),
        ],
        compiler_params=pltpu.CompilerParams(
            dimension_semantics=("parallel", "arbitrary"), vmem_limit_bytes=VMEM_LIMIT_BYTES),
    )(x, norm_g, w, cos_t, sin_t, ng, dmat, kend, qdec, cdec, gm, qmask, vmask, smask)


def _ffn_kernel(x_ref, ya_ref, yb_ref, yc_ref, yd_ref, wo_ref, ng_ref, wg_ref, wv_ref, cwg_ref, cwv_ref,
                cbg_ref, cbv_ref, wd_ref, o_ref, h_s, ug_s, uv_s, cg_s, cv_s, *, tile):
    t = pl.program_id(1)
    f = pl.program_id(2)
    gw = GROUP_WIDTH

    @pl.when(f == 0)
    def _():
        x1 = x_ref[...]
        for j, y_ref in enumerate((ya_ref, yb_ref, yc_ref, yd_ref)):
            x1 = x1 + jnp.dot(y_ref[...], wo_ref[j * gw:(j + 1) * gw, :], preferred_element_type=F32)
        o_ref[...] = x1
        h_s[...] = _rms_rows(x1, ng_ref[...]).astype(BF16)

    h = h_s[...]

    def conv_branch(w_ref, cw_ref, cb_ref, u_s, c_s):
        u = jnp.dot(h, w_ref[...], preferred_element_type=F32)
        u_s[SUBLANES:SUBLANES + tile, :] = u

        @pl.when(t == 0)
        def _():
            u_s[0:SUBLANES, :] = jnp.zeros((SUBLANES, u.shape[1]), F32)

        @pl.when(t > 0)
        def _():
            u_s[0:SUBLANES, :] = c_s[f]

        cw = cw_ref[...]
        y = cb_ref[...] + cw[FFN_CONV - 1:FFN_CONV, :] * u
        for j in range(FFN_CONV - 1):
            y = y + cw[j:j + 1, :] * u_s[pl.ds(SUBLANES - (FFN_CONV - 1) + j, tile), :]
        c_s[f] = u[tile - SUBLANES:tile, :]
        return y

    gate = conv_branch(wg_ref, cwg_ref, cbg_ref, ug_s, cg_s)
    val = conv_branch(wv_ref, cwv_ref, cbv_ref, uv_s, cv_s)
    act = (_silu(gate) * val).astype(BF16)
    o_ref[...] += jnp.dot(act, wd_ref[...], preferred_element_type=F32)


def _ffn_call(x, ya, yb, yc, yd, wo, ng, w_up, conv_w, conv_b, w_down):
    bsz, s, d = x.shape
    tile = min(FFN_TILE, s)
    cols = FFN_COLS
    nf = FFN_DIM // cols
    gw = GROUP_WIDTH
    row = lambda b, t, f: (b, t, 0)
    const = lambda b, t, f: (0, 0)
    lo = lambda b, t, f: (0, f)
    hi = lambda b, t, f: (0, f + nf)
    return pl.pallas_call(
        functools.partial(_ffn_kernel, tile=tile),
        grid=(bsz, s // tile, nf),
        in_specs=[
            pl.BlockSpec((None, tile, d), row),
            pl.BlockSpec((None, tile, gw), row),
            pl.BlockSpec((None, tile, gw), row),
            pl.BlockSpec((None, tile, gw), row),
            pl.BlockSpec((None, tile, gw), row),
            pl.BlockSpec((d, d), const),
            pl.BlockSpec((1, d), const),
            pl.BlockSpec((d, cols), lo),
            pl.BlockSpec((d, cols), hi),
            pl.BlockSpec((FFN_CONV, cols), lo),
            pl.BlockSpec((FFN_CONV, cols), hi),
            pl.BlockSpec((1, cols), lo),
            pl.BlockSpec((1, cols), hi),
            pl.BlockSpec((cols, d), lambda b, t, f: (f, 0)),
        ],
        out_specs=pl.BlockSpec((None, tile, d), row),
        out_shape=jax.ShapeDtypeStruct((bsz, s, d), F32),
        scratch_shapes=[
            pltpu.VMEM((tile, d), BF16),
            pltpu.VMEM((tile + SUBLANES, cols), F32),
            pltpu.VMEM((tile + SUBLANES, cols), F32),
            pltpu.VMEM((nf, SUBLANES, cols), F32),
            pltpu.VMEM((nf, SUBLANES, cols), F32),
        ],
        compiler_params=pltpu.CompilerParams(
            dimension_semantics=("parallel", "arbitrary", "arbitrary"), vmem_limit_bytes=VMEM_LIMIT_BYTES),
    )(x, ya, yb, yc, yd, wo, ng, w_up, w_up, conv_w, conv_w, conv_b, conv_b, w_down)


def _lane_group_mask(n_groups, width, rows=SUBLANES):
    lane = np.arange(n_groups * width) // width
    m = np.zeros((rows, n_groups * width), np.float32)
    for g in range(n_groups):
        m[g] = (lane == g)
    return jnp.asarray(m)


def _block_diag_mask(n, rows_per, cols_per):
    r = np.arange(n * rows_per)[:, None] // rows_per
    c = np.arange(n * cols_per)[None, :] // cols_per
    return jnp.asarray((r == c).astype(np.float32))


def _rope_tables(s, inv_freq, n_heads):
    ang = jnp.arange(s, dtype=F32)[:, None] * inv_freq[None, :]
    cos = jnp.cos(ang)
    sin = jnp.sin(ang)
    cos_t = jnp.tile(jnp.concatenate([cos, cos], axis=-1), (1, n_heads))
    sin_t = jnp.tile(jnp.concatenate([-sin, sin], axis=-1), (1, n_heads))
    return cos_t, sin_t


def kernel(x, attn_norm_g, w_in, moba_q_norm_g, moba_k_norm_g, ssm_conv_w, ssm_conv_b, ssm_dt_bias, ssm_a_log,
           ssm_d, ssm_norm_g, gla_gate_w2, gla_gate_b, gla_norm_g, ret_norm_g, w_out, ffn_norm_g, ffn_w_up,
           ffn_conv_w, ffn_conv_b, ffn_w_down):
    bsz, s, d = x.shape
    depth = w_in.shape[0]
    gw = GROUP_WIDTH

    moba_inv = ROPE_THETA ** (-jnp.arange(0, MOBA_HEAD_DIM, 2, dtype=F32) / MOBA_HEAD_DIM)
    moba_cos, moba_sin = _rope_tables(s, moba_inv, MOBA_HEADS)
    ret_inv = 1.0 / (ROPE_THETA ** jnp.linspace(0.0, 1.0, RET_DK // 2, dtype=F32))
    ret_cos, ret_sin = _rope_tables(s, ret_inv, RET_HEADS)
    head_mean = _block_diag_mask(4, 64, 64) / 64.0
    hmask256 = _lane_group_mask(4, 64)
    qmask128 = _lane_group_mask(4, 32)
    ssm_gmask = _lane_group_mask(SSM_GROUPS, SSM_STATE)
    ssm_smask = _block_diag_mask(SSM_GROUPS, SSM_STATE, 2 * SSM_HEAD_DIM)
    lin_smask = _block_diag_mask(4, 32, 64)
    tri128 = jnp.asarray(np.tril(np.ones((SSM_CHUNK, SSM_CHUNK), np.float32)))
    P = 2 * GLA_CHUNK
    same_chunk = (np.arange(P)[:, None] // GLA_CHUNK) == (np.arange(P)[None, :] // GLA_CHUNK)
    gla_tri = jnp.asarray((np.tril(np.ones((P, P))) * same_chunk).astype(np.float32))
    gla_ones = jnp.asarray(same_chunk.astype(np.float32))

    L = RET_CHUNK
    ret_lg = jnp.log(1.0 - 2.0 ** (-5.0 - jnp.arange(RET_HEADS, dtype=F32)))
    idx = jnp.arange(L, dtype=F32)
    diff = idx[:, None] - idx[None, :]
    ret_dmat = jnp.where(diff >= 0, jnp.exp(jnp.maximum(diff, 0.0)[None] * ret_lg[:, None, None]), 0.0)
    ret_kend = jnp.repeat(jnp.exp((L - 1 - idx)[:, None] * ret_lg[None, :]), RET_DK, axis=1)
    ret_qdec = jnp.repeat(jnp.exp((idx + 1.0)[:, None] * ret_lg[None, :]), RET_DK, axis=1)
    ret_cdec = jnp.repeat(jnp.exp(L * ret_lg), RET_DV)[None, :]

    o_mq, o_mk, o_mv = 0, 256, 512
    o_sz, o_sx, o_sd = 768, 1024, 1536
    o_gq, o_gk, o_gv, o_gr, o_gg = 1540, 1668, 1796, 2052, 2308
    o_rq = 2324

    for l in range(depth):
        wl = w_in[l]
        w_moba = wl[:, o_mq:o_sz].astype(BF16)
        w_ssm = jnp.concatenate(
            [wl[:, o_sz:o_sd], jnp.repeat(wl[:, o_sd:o_sd + SSM_HEADS], SSM_HEAD_DIM, axis=1)], axis=1).astype(BF16)
        w_gla = jnp.concatenate(
            [wl[:, o_gq:o_gg + GLA_GATE_RANK], jnp.zeros((d, LANES - GLA_GATE_RANK), F32)], axis=1).astype(BF16)
        w_ret = wl[:, o_rq:o_rq + 768].astype(BF16)
        ng = attn_norm_g[l][None, :]

        y_moba = _moba_call(x, ng, w_moba, jnp.tile(moba_q_norm_g[l], MOBA_HEADS)[None, :],
                            jnp.tile(moba_k_norm_g[l], MOBA_HEADS)[None, :], moba_cos, moba_sin, head_mean,
                            hmask256)
        y_ssm = _ssd_call(x, ng, w_ssm, ssm_conv_w[l], ssm_conv_b[l][None, :],
                          jnp.repeat(ssm_dt_bias[l], SSM_HEAD_DIM)[None, :],
                          jnp.repeat(ssm_a_log[l], SSM_HEAD_DIM)[None, :],
                          jnp.repeat(ssm_d[l], SSM_HEAD_DIM)[None, :], ssm_norm_g[l][None, :], tri128, ssm_gmask,
                          hmask256, ssm_smask)
        w2 = jnp.concatenate([gla_gate_w2[l], jnp.zeros((LANES - GLA_GATE_RANK, GLA_HEADS * GLA_DK), F32)],
                             axis=0).astype(BF16)
        y_gla = _gla_call(x, ng, w_gla, w2, gla_gate_b[l][None, :], jnp.tile(gla_norm_g[l], GLA_HEADS)[None, :],
                          gla_tri, gla_ones, head_mean, qmask128, hmask256, lin_smask)
        y_ret = _ret_call(x, ng, w_ret, ret_cos, ret_sin, jnp.tile(ret_norm_g[l], RET_HEADS)[None, :], ret_dmat,
                          ret_kend, ret_qdec, ret_cdec, head_mean, qmask128, hmask256, lin_smask)
        x = _ffn_call(x, y_moba, y_ssm, y_gla, y_ret, w_out[l].astype(BF16), ffn_norm_g[l][None, :],
                      ffn_w_up[l].astype(BF16), ffn_conv_w[l], ffn_conv_b[l][None, :], ffn_w_down[l].astype(BF16))
    return x
```

```python
import functools

import numpy as np
import jax
import jax.numpy as jnp
from jax import lax
from jax.experimental import pallas as pl
from jax.experimental.pallas import tpu as pltpu

F32 = jnp.float32
BF16 = jnp.bfloat16
HIGHEST = lax.Precision.HIGHEST

D_MODEL = 1024
GROUP_WIDTH = 256
NORM_EPS = 1e-6
NEG_INF = -1e30

MOBA_HEADS = 4
MOBA_HEAD_DIM = 64
MOBA_BLOCK = 256
MOBA_TOPK = 3
ROPE_THETA = 10000.0

SSM_HEADS = 4
SSM_HEAD_DIM = 64
SSM_GROUPS = 2
SSM_STATE = 64
SSM_CONV = 4
SSM_CHUNK = 128
SSM_CONV_CH = 512

GLA_HEADS = 4
GLA_DV = 64
GLA_DK = 32
GLA_GATE_RANK = 16
GLA_GATE_TAU = 16.0
GLA_CHUNK = 64

RET_HEADS = 4
RET_DV = 64
RET_DK = 32
RET_CHUNK = 128

FFN_DIM = 2816
FFN_CONV = 3

LANES = 128
SUBLANES = 8
VMEM_LIMIT_BYTES = 48 * 1024 * 1024

MIX_TILE = 512
FFN_TILE = 512
FFN_COLS = 256


def _rms_rows(x, g):
    ms = jnp.mean(x * x, axis=-1, keepdims=True)
    return x * lax.rsqrt(ms + NORM_EPS) * g


def _dot(a, b):
    return jnp.dot(a.astype(BF16), b.astype(BF16), preferred_element_type=F32)


def _dot_nt(a, b):
    return lax.dot_general(a.astype(BF16), b.astype(BF16), (((1,), (1,)), ((), ())),
                           preferred_element_type=F32)


def _dot_f32(a, b):
    return jnp.dot(a, b, precision=HIGHEST, preferred_element_type=F32)


def _sigmoid(x):
    return 1.0 / (1.0 + jnp.exp(-x))


def _silu(x):
    return x * _sigmoid(x)


def _softplus(x):
    return jnp.maximum(x, 0.0) + jnp.log1p(jnp.exp(-jnp.abs(x)))


def _rotate_half(t, half):
    width = t.shape[-1]
    lane = lax.broadcasted_iota(jnp.int32, t.shape, t.ndim - 1)
    first = (lane % (2 * half)) < half
    return jnp.where(first, pltpu.roll(t, width - half, t.ndim - 1), pltpu.roll(t, half, t.ndim - 1))


def _moba_kernel(x_ref, g_ref, w_ref, qg_ref, kg_ref, cos_ref, sin_ref, gm_ref, hmask_ref, y_ref,
                 k_s, vt_s, km_s, qm_s, bias_s, m_s, l_s, acc_s, *, n_blk):
    i = pl.program_id(1)
    blk = MOBA_BLOCK
    gw = GROUP_WIDTH
    dh = MOBA_HEAD_DIM

    h = _rms_rows(x_ref[...], g_ref[...]).astype(BF16)
    p = jnp.dot(h, w_ref[...], preferred_element_type=F32)
    gm = gm_ref[...]
    cos = cos_ref[...]
    sin = sin_ref[...]

    def prep(t, gain):
        ms = _dot_f32(t * t, gm)
        t = t * lax.rsqrt(ms + NORM_EPS) * gain
        return t * cos + _rotate_half(t, dh // 2) * sin

    q = prep(p[:, 0:gw], qg_ref[...])
    k = prep(p[:, gw:2 * gw], kg_ref[...])
    v = p[:, 2 * gw:3 * gw]

    @pl.when(i == 0)
    def _():
        km_s[...] = jnp.zeros_like(km_s)

    kb = k.astype(BF16)
    vt = v.T.astype(BF16)
    k_s[i] = kb
    vt_s[i] = vt
    km_s[i] = jnp.broadcast_to(jnp.mean(k, axis=0, keepdims=True), (SUBLANES, gw))

    row8 = lax.broadcasted_iota(jnp.int32, (SUBLANES, gw), 0)
    kmat = jnp.zeros((SUBLANES, gw), F32)
    for n in range(n_blk):
        kmat = jnp.where(row8 == n, km_s[n], kmat)
    hmask = hmask_ref[...]
    kmh = jnp.concatenate([kmat * hmask[hd:hd + 1, :] for hd in range(MOBA_HEADS)], axis=0)
    gate = lax.dot_general(kmh, q, (((1,), (1,)), ((), ())), precision=HIGHEST,
                           preferred_element_type=F32)

    rowq = lax.broadcasted_iota(jnp.int32, (SUBLANES, blk), 0)
    valid = rowq < i
    bias_tiles = [jnp.zeros((SUBLANES, blk), F32) for _ in range(n_blk)]
    for hd in range(MOBA_HEADS):
        gh = gate[hd * SUBLANES:(hd + 1) * SUBLANES, :]
        for n in range(n_blk):
            gn = gh[n:n + 1, :]
            beats = valid & ((gh > gn) | ((gh == gn) & (rowq < n)))
            cnt = jnp.sum(beats.astype(F32), axis=0, keepdims=True)
            bias_n = jnp.where(cnt < float(MOBA_TOPK), 0.0, NEG_INF)
            bias_tiles[n] = jnp.where(rowq == hd, bias_n, bias_tiles[n])
    for n in range(n_blk):
        bias_s[n] = bias_tiles[n]

    rj = lax.broadcasted_iota(jnp.int32, (blk, blk), 0)
    cq = lax.broadcasted_iota(jnp.int32, (blk, blk), 1)
    causal = rj <= cq
    scale = dh ** -0.5
    for hd in range(MOBA_HEADS):
        qh = (q * (scale * hmask[hd:hd + 1, :])).astype(BF16)
        qm_s[hd] = qh
        s = jnp.where(causal, _dot_nt(kb, qh), NEG_INF)
        m = jnp.max(s, axis=0, keepdims=True)
        pr = jnp.exp(s - m)
        m_s[hd:hd + 1, :] = m
        l_s[hd:hd + 1, :] = jnp.sum(pr, axis=0, keepdims=True)
        acc_s[hd * dh:(hd + 1) * dh, :] = jnp.dot(vt[hd * dh:(hd + 1) * dh, :], pr.astype(BF16),
                                                  preferred_element_type=F32)

    def past_block(n, carry):
        kn = k_s[n]
        vtn = vt_s[n]
        bt = bias_s[n]
        for hd in range(MOBA_HEADS):
            s = _dot_nt(kn, qm_s[hd]) + bt[hd:hd + 1, :]
            m_old = m_s[hd:hd + 1, :]
            m_new = jnp.maximum(m_old, jnp.max(s, axis=0, keepdims=True))
            alpha = jnp.exp(m_old - m_new)
            pr = jnp.exp(s - m_new)
            l_s[hd:hd + 1, :] = alpha * l_s[hd:hd + 1, :] + jnp.sum(pr, axis=0, keepdims=True)
            acc_s[hd * dh:(hd + 1) * dh, :] = alpha * acc_s[hd * dh:(hd + 1) * dh, :] + jnp.dot(
                vtn[hd * dh:(hd + 1) * dh, :], pr.astype(BF16), preferred_element_type=F32)
            m_s[hd:hd + 1, :] = m_new
        return carry

    lax.fori_loop(0, i, past_block, 0)

    o_t = jnp.concatenate(
        [acc_s[hd * dh:(hd + 1) * dh, :] * (1.0 / l_s[hd:hd + 1, :]) for hd in range(MOBA_HEADS)], axis=0)
    y_ref[...] = o_t.T.astype(y_ref.dtype)


def _moba_call(x, norm_g, w, qg, kg, cos_t, sin_t, gm, hmask):
    bsz, s, d = x.shape
    blk = MOBA_BLOCK
    n_blk = s // blk
    gw = GROUP_WIDTH
    const = lambda b, i: (0, 0)
    return pl.pallas_call(
        functools.partial(_moba_kernel, n_blk=n_blk),
        grid=(bsz, n_blk),
        in_specs=[
            pl.BlockSpec((None, blk, d), lambda b, i: (b, i, 0)),
            pl.BlockSpec((1, d), const),
            pl.BlockSpec((d, 3 * gw), const),
            pl.BlockSpec((1, gw), const),
            pl.BlockSpec((1, gw), const),
            pl.BlockSpec((blk, gw), lambda b, i: (i, 0)),
            pl.BlockSpec((blk, gw), lambda b, i: (i, 0)),
            pl.BlockSpec((gw, gw), const),
            pl.BlockSpec((SUBLANES, gw), const),
        ],
        out_specs=pl.BlockSpec((None, blk, gw), lambda b, i: (b, i, 0)),
        out_shape=jax.ShapeDtypeStruct((bsz, s, gw), BF16),
        scratch_shapes=[
            pltpu.VMEM((n_blk, blk, gw), BF16),
            pltpu.VMEM((n_blk, gw, blk), BF16),
            pltpu.VMEM((n_blk, SUBLANES, gw), F32),
            pltpu.VMEM((MOBA_HEADS, blk, gw), BF16),
            pltpu.VMEM((n_blk, SUBLANES, blk), F32),
            pltpu.VMEM((SUBLANES, blk), F32),
            pltpu.VMEM((SUBLANES, blk), F32),
            pltpu.VMEM((gw, blk), F32),
        ],
        compiler_params=pltpu.CompilerParams(
            dimension_semantics=("parallel", "arbitrary"), vmem_limit_bytes=VMEM_LIMIT_BYTES),
    )(x, norm_g, w, qg, kg, cos_t, sin_t, gm, hmask)


def _ssd_kernel(x_ref, g_ref, w_ref, cw_ref, cb_ref, dtb_ref, alog_ref, dsk_ref, ng_ref, tri_ref,
                gmask_ref, hmask_ref, smask_ref, y_ref, xp_s, st_s, y_s, *, tile):
    t = pl.program_id(1)
    gw = GROUP_WIDTH
    L = SSM_CHUNK
    gn = SSM_GROUPS * SSM_STATE

    @pl.when(t == 0)
    def _():
        xp_s[0:SUBLANES, :] = jnp.zeros((SUBLANES, SSM_CONV_CH), F32)
        st_s[...] = jnp.zeros_like(st_s)

    h = _rms_rows(x_ref[...], g_ref[...]).astype(BF16)
    p = jnp.dot(h, w_ref[...], preferred_element_type=F32)
    z = p[:, 0:gw]
    xp_s[SUBLANES:SUBLANES + tile, :] = p[:, gw:gw + SSM_CONV_CH]
    cw = cw_ref[...]
    conv = cb_ref[...]
    for j in range(SSM_CONV):
        conv = conv + cw[j:j + 1, :] * xp_s[pl.ds(SUBLANES - (SSM_CONV - 1) + j, tile), :]
    xp_s[0:SUBLANES, :] = xp_s[tile:tile + SUBLANES, :]
    xbc = _silu(conv)
    xs = xbc[:, 0:gw]
    bm = xbc[:, gw:gw + gn]
    cm = xbc[:, gw + gn:gw + 2 * gn]
    dt = _softplus(p[:, gw + SSM_CONV_CH:2 * gw + SSM_CONV_CH] + dtb_ref[...])
    a = dt * (-jnp.exp(alog_ref[...]))
    xdt = xs * dt

    tri = tri_ref[...]
    gmask = gmask_ref[...]
    hmask = hmask_ref[...]
    smask = smask_ref[...]
    rl = lax.broadcasted_iota(jnp.int32, (L, L), 0)
    cs = lax.broadcasted_iota(jnp.int32, (L, L), 1)
    causal = rl >= cs
    rep = SSM_HEADS // SSM_GROUPS
    for c in range(tile // L):
        sl = slice(c * L, (c + 1) * L)
        a_cs = _dot_f32(tri, a[sl])
        a_cs_t = a_cs.T
        a_last = a_cs[L - 1:L, :]
        cmc = cm[sl]
        bmc = bm[sl]
        xdtc = xdt[sl]
        y = jnp.zeros((L, gw), F32)
        for g in range(SSM_GROUPS):
            scores = _dot_nt(cmc * gmask[g:g + 1, :], bmc)
            for hd in range(g * rep, (g + 1) * rep):
                col = a_cs[:, hd * SSM_HEAD_DIM:hd * SSM_HEAD_DIM + 1]
                row = a_cs_t[hd * SSM_HEAD_DIM:hd * SSM_HEAD_DIM + 1, :]
                decay = jnp.where(causal, jnp.exp(col - row), 0.0)
                y = y + _dot(scores * decay, xdtc * hmask[hd:hd + 1, :])
        state = st_s[...]
        y = y + _dot(cmc, state) * jnp.exp(a_cs)
        new = _dot(bmc.T, xdtc * jnp.exp(a_last - a_cs)) * smask
        st_s[...] = state * jnp.exp(a_last) + new
        y_s[sl, :] = y

    y = (y_s[...] + xs * dsk_ref[...]) * _silu(z)
    ng = ng_ref[...]
    half = gw // SSM_GROUPS
    for g in range(SSM_GROUPS):
        yg = y[:, g * half:(g + 1) * half]
        ms = jnp.mean(yg * yg, axis=-1, keepdims=True)
        y_ref[:, g * half:(g + 1) * half] = (yg * lax.rsqrt(ms + NORM_EPS) * ng[:, g * half:(g + 1) * half]
                                             ).astype(y_ref.dtype)


def _ssd_call(x, norm_g, w, cw, cb, dtb, alog, dsk, ng, tri, gmask, hmask, smask):
    bsz, s, d = x.shape
    tile = min(MIX_TILE, s)
    gw = GROUP_WIDTH
    wcols = w.shape[1]
    gn = SSM_GROUPS * SSM_STATE
    const = lambda b, t: (0, 0)
    return pl.pallas_call(
        functools.partial(_ssd_kernel, tile=tile),
        grid=(bsz, s // tile),
        in_specs=[
            pl.BlockSpec((None, tile, d), lambda b, t: (b, t, 0)),
            pl.BlockSpec((1, d), const),
            pl.BlockSpec((d, wcols), const),
            pl.BlockSpec((SSM_CONV, SSM_CONV_CH), const),
            pl.BlockSpec((1, SSM_CONV_CH), const),
            pl.BlockSpec((1, gw), const),
            pl.BlockSpec((1, gw), const),
            pl.BlockSpec((1, gw), const),
            pl.BlockSpec((1, gw), const),
            pl.BlockSpec((SSM_CHUNK, SSM_CHUNK), const),
            pl.BlockSpec((SUBLANES, gn), const),
            pl.BlockSpec((SUBLANES, gw), const),
            pl.BlockSpec((gn, gw), const),
        ],
        out_specs=pl.BlockSpec((None, tile, gw), lambda b, t: (b, t, 0)),
        out_shape=jax.ShapeDtypeStruct((bsz, s, gw), BF16),
        scratch_shapes=[
            pltpu.VMEM((tile + SUBLANES, SSM_CONV_CH), F32),
            pltpu.VMEM((gn, gw), F32),
            pltpu.VMEM((tile, gw), F32),
        ],
        compiler_params=pltpu.CompilerParams(
            dimension_semantics=("parallel", "arbitrary"), vmem_limit_bytes=VMEM_LIMIT_BYTES),
    )(x, norm_g, w, cw, cb, dtb, alog, dsk, ng, tri, gmask, hmask, smask)


def _gla_kernel(x_ref, g_ref, w_ref, w2_ref, b2_ref, ng_ref, tri_ref, ones_ref, gm_ref, qmask_ref,
                vmask_ref, smask_ref, y_ref, st_s, o_s, *, tile):
    t = pl.program_id(1)
    gw = GROUP_WIDTH
    L = GLA_CHUNK
    P = 2 * L
    qk = GLA_HEADS * GLA_DK

    @pl.when(t == 0)
    def _():
        st_s[...] = jnp.zeros_like(st_s)

    h = _rms_rows(x_ref[...], g_ref[...]).astype(BF16)
    p = jnp.dot(h, w_ref[...], preferred_element_type=F32)
    q = p[:, 0:qk] * (GLA_DK ** -0.5)
    k = p[:, qk:2 * qk]
    v = p[:, 2 * qk:2 * qk + gw]
    r = p[:, 2 * qk + gw:2 * qk + 2 * gw]
    g_pre = _dot(p[:, 2 * qk + 2 * gw:2 * qk + 2 * gw + LANES], w2_ref[...]) + b2_ref[...]
    lg = (jnp.minimum(g_pre, 0.0) - jnp.log1p(jnp.exp(-jnp.abs(g_pre)))) / GLA_GATE_TAU

    tri = tri_ref[...]
    ones_bd = ones_ref[...]
    qmask = qmask_ref[...]
    vmask = vmask_ref[...]
    smask = smask_ref[...]
    rl = lax.broadcasted_iota(jnp.int32, (P, P), 0)
    cs = lax.broadcasted_iota(jnp.int32, (P, P), 1)
    causal = (rl >= cs) & ((rl // L) == (cs // L))
    lane_p = lax.broadcasted_iota(jnp.int32, (qk, P), 1)
    for c in range(tile // P):
        sl = slice(c * P, (c + 1) * P)
        gc = lg[sl]
        bcs = _dot_f32(tri, gc)
        b_last = _dot_f32(ones_bd, gc)
        qd = q[sl] * jnp.exp(bcs)
        ki = k[sl] * jnp.exp(-bcs)
        ke_t = (k[sl] * jnp.exp(b_last - bcs)).T
        dec_t = jnp.exp(b_last).T
        vc = v[sl]
        o = jnp.zeros((P, gw), F32)
        for hd in range(GLA_HEADS):
            att = jnp.where(causal, _dot_nt(qd * qmask[hd:hd + 1, :], ki), 0.0)
            o = o + _dot(att, vc * vmask[hd:hd + 1, :])
        o_s[sl, :] = o
        for j in range(2):
            rows = slice(c * P + j * L, c * P + (j + 1) * L)
            state = st_s[...]
            o_s[rows, :] = o_s[rows, :] + _dot(qd[j * L:(j + 1) * L], state)
            in_chunk = (lane_p // L) == j
            new = _dot(jnp.where(in_chunk, ke_t, 0.0), vc) * smask
            st_s[...] = state * dec_t[:, j * L:j * L + 1] + new

    o = o_s[...]
    ms = _dot_f32(o * o, gm_ref[...])
    y_ref[...] = (o * lax.rsqrt(ms + NORM_EPS) * ng_ref[...] * _silu(r)).astype(y_ref.dtype)


def _gla_call(x, norm_g, w, w2, b2, ng, tri, ones_bd, gm, qmask, vmask, smask):
    bsz, s, d = x.shape
    tile = min(MIX_TILE, s)
    gw = GROUP_WIDTH
    qk = GLA_HEADS * GLA_DK
    P = 2 * GLA_CHUNK
    const = lambda b, t: (0, 0)
    return pl.pallas_call(
        functools.partial(_gla_kernel, tile=tile),
        grid=(bsz, s // tile),
        in_specs=[
            pl.BlockSpec((None, tile, d), lambda b, t: (b, t, 0)),
            pl.BlockSpec((1, d), const),
            pl.BlockSpec((d, w.shape[1]), const),
            pl.BlockSpec((LANES, qk), const),
            pl.BlockSpec((1, qk), const),
            pl.BlockSpec((1, gw), const),
            pl.BlockSpec((P, P), const),
            pl.BlockSpec((P, P), const),
            pl.BlockSpec((gw, gw), const),
            pl.BlockSpec((SUBLANES, qk), const),
            pl.BlockSpec((SUBLANES, gw), const),
            pl.BlockSpec((qk, gw), const),
        ],
        out_specs=pl.BlockSpec((None, tile, gw), lambda b, t: (b, t, 0)),
        out_shape=jax.ShapeDtypeStruct((bsz, s, gw), BF16),
        scratch_shapes=[
            pltpu.VMEM((qk, gw), F32),
            pltpu.VMEM((tile, gw), F32),
        ],
        compiler_params=pltpu.CompilerParams(
            dimension_semantics=("parallel", "arbitrary"), vmem_limit_bytes=VMEM_LIMIT_BYTES),
    )(x, norm_g, w, w2, b2, ng, tri, ones_bd, gm, qmask, vmask, smask)


def _ret_kernel(x_ref, g_ref, w_ref, cos_ref, sin_ref, ng_ref, dmat_ref, kend_ref, qdec_ref, cdec_ref,
                gm_ref, qmask_ref, vmask_ref, smask_ref, y_ref, st_s, o_s, *, tile):
    t = pl.program_id(1)
    gw = GROUP_WIDTH
    L = RET_CHUNK
    qk = RET_HEADS * RET_DK

    @pl.when(t == 0)
    def _():
        st_s[...] = jnp.zeros_like(st_s)

    h = _rms_rows(x_ref[...], g_ref[...]).astype(BF16)
    p = jnp.dot(h, w_ref[...], preferred_element_type=F32)
    cos = cos_ref[...]
    sin = sin_ref[...]

    def rope(u):
        return u * cos + _rotate_half(u, RET_DK // 2) * sin

    q = rope(p[:, 0:qk])
    k = rope(p[:, qk:2 * qk]) * (RET_DK ** -0.5)
    v = p[:, 2 * qk:2 * qk + gw]
    gate = p[:, 2 * qk + gw:2 * qk + 2 * gw]

    kend = kend_ref[...]
    qdec = qdec_ref[...]
    cdec = cdec_ref[...]
    qmask = qmask_ref[...]
    vmask = vmask_ref[...]
    smask = smask_ref[...]
    for c in range(tile // L):
        sl = slice(c * L, (c + 1) * L)
        qc = q[sl]
        kc = k[sl]
        vc = v[sl]
        o = jnp.zeros((L, gw), F32)
        for hd in range(RET_HEADS):
            att = _dot_nt(qc * qmask[hd:hd + 1, :], kc) * dmat_ref[hd]
            o = o + _dot(att, vc * vmask[hd:hd + 1, :])
        state = st_s[...]
        o = o + _dot(qc * qdec, state)
        new = _dot((kc * kend).T, vc) * smask
        st_s[...] = state * cdec + new
        o_s[sl, :] = o

    o = o_s[...]
    ms = _dot_f32(o * o, gm_ref[...])
    y_ref[...] = (o * lax.rsqrt(ms + NORM_EPS) * ng_ref[...] * _silu(gate)).astype(y_ref.dtype)


def _ret_call(x, norm_g, w, cos_t, sin_t, ng, dmat, kend, qdec, cdec, gm, qmask, vmask, smask):
    bsz, s, d = x.shape
    tile = min(MIX_TILE, s)
    gw = GROUP_WIDTH
    qk = RET_HEADS * RET_DK
    L = RET_CHUNK
    const = lambda b, t: (0, 0)
    return pl.pallas_call(
        functools.partial(_ret_kernel, tile=tile),
        grid=(bsz, s // tile),
        in_specs=[
            pl.BlockSpec((None, tile, d), lambda b, t: (b, t, 0)),
            pl.BlockSpec((1, d), const),
            pl.BlockSpec((d, w.shape[1]), const),
            pl.BlockSpec((tile, qk), lambda b, t: (t, 0)),
            pl.BlockSpec((tile, qk), lambda b, t: (t, 0)),
            pl.BlockSpec((1, gw), const),
            pl.BlockSpec((RET_HEADS, L, L), lambda b, t: (0, 0, 0)),
            pl.BlockSpec((L, qk), const),
            pl.BlockSpec((L, qk), const),
            pl.BlockSpec((1, gw), const),
            pl.BlockSpec((gw, gw), const),
            pl.BlockSpec((SUBLANES, qk), const),
            pl.BlockSpec((SUBLANES, gw), const),
            pl.BlockSpec((qk, gw), const),
        ],
        out_specs=pl.BlockSpec((None, tile, gw), lambda b, t: (b, t, 0)),
        out_shape=jax.ShapeDtypeStruct((bsz, s, gw), BF16),
        scratch_shapes=[
            pltpu.VMEM((qk, gw), F32),
            pltpu.VMEM((tile, gw), F32),
        ],
        compiler_params=pltpu.CompilerParams(
            dimension_semantics=("parallel", "arbitrary"), vmem_limit_bytes=VMEM_LIMIT_BYTES),
    )(x, norm_g, w, cos_t, sin_t, ng, dmat, kend, qdec, cdec, gm, qmask, vmask, smask)


def _ffn_kernel(x_ref, ya_ref, yb_ref, yc_ref, yd_ref, wo_ref, ng_ref, wg_ref, wv_ref, cwg_ref, cwv_ref,
                cbg_ref, cbv_ref, wd_ref, o_ref, h_s, ug_a, uv_a, ug_b, uv_b, cg_s, cv_s, *, tile, nf):
    t = pl.program_id(1)
    gw = GROUP_WIDTH
    halo = SUBLANES

    @pl.when(t == 0)
    def _():
        cg_s[...] = jnp.zeros_like(cg_s)
        cv_s[...] = jnp.zeros_like(cv_s)

    x1 = x_ref[...]
    for j, y_ref in enumerate((ya_ref, yb_ref, yc_ref, yd_ref)):
        x1 = x1 + jnp.dot(y_ref[...], wo_ref[j * gw:(j + 1) * gw, :], preferred_element_type=F32)
    o_ref[...] = x1
    h_s[...] = _rms_rows(x1, ng_ref[...]).astype(BF16)

    def up(f, ug_s, uv_s):
        h = h_s[...]
        for w_ref, u_s, c_s in ((wg_ref, ug_s, cg_s), (wv_ref, uv_s, cv_s)):
            u_s[halo:halo + tile, :] = jnp.dot(h, w_ref[f], preferred_element_type=F32)
            u_s[0:halo, :] = c_s[f]
            c_s[f] = u_s[tile:tile + halo, :]

    def down(f, ug_s, uv_s):
        def conv(u_s, cw_ref, cb_ref):
            cw = cw_ref[f]
            y = cb_ref[f] + cw[FFN_CONV - 1:FFN_CONV, :] * u_s[halo:halo + tile, :]
            for j in range(FFN_CONV - 1):
                y = y + cw[j:j + 1, :] * u_s[pl.ds(halo - (FFN_CONV - 1) + j, tile), :]
            return y

        act = (_silu(conv(ug_s, cwg_ref, cbg_ref)) * conv(uv_s, cwv_ref, cbv_ref)).astype(BF16)
        o_ref[...] += jnp.dot(act, wd_ref[f], preferred_element_type=F32)

    up(0, ug_a, uv_a)

    def pair(j, carry):
        f = 2 * j
        up(f + 1, ug_b, uv_b)
        down(f, ug_a, uv_a)
        up(f + 2, ug_a, uv_a)
        down(f + 1, ug_b, uv_b)
        return carry

    lax.fori_loop(0, (nf - 1) // 2, pair, 0)
    down(nf - 1, ug_a, uv_a)


def _ffn_call(x, ya, yb, yc, yd, wo, ng, w_up, conv_w, conv_b, w_down):
    bsz, s, d = x.shape
    tile = min(FFN_TILE, s)
    cols = FFN_COLS
    nf = FFN_DIM // cols
    assert nf % 2 == 1 and nf * cols == FFN_DIM
    gw = GROUP_WIDTH
    w_up_t = w_up.reshape(d, 2 * nf, cols).transpose(1, 0, 2)
    cw_t = conv_w.reshape(FFN_CONV, 2 * nf, cols).transpose(1, 0, 2)
    cb_t = conv_b.reshape(2 * nf, 1, cols)
    wd_t = w_down.reshape(nf, cols, d)
    row = lambda b, t: (b, t, 0)
    const2 = lambda b, t: (0, 0)
    lo = lambda b, t: (0, 0, 0)
    hi = lambda b, t: (1, 0, 0)
    once = pl.Buffered(1)
    return pl.pallas_call(
        functools.partial(_ffn_kernel, tile=tile, nf=nf),
        grid=(bsz, s // tile),
        in_specs=[
            pl.BlockSpec((None, tile, d), row),
            pl.BlockSpec((None, tile, gw), row),
            pl.BlockSpec((None, tile, gw), row),
            pl.BlockSpec((None, tile, gw), row),
            pl.BlockSpec((None, tile, gw), row),
            pl.BlockSpec((d, d), const2, pipeline_mode=once),
            pl.BlockSpec((1, d), const2),
            pl.BlockSpec((nf, d, cols), lo, pipeline_mode=once),
            pl.BlockSpec((nf, d, cols), hi, pipeline_mode=once),
            pl.BlockSpec((nf, FFN_CONV, cols), lo),
            pl.BlockSpec((nf, FFN_CONV, cols), hi),
            pl.BlockSpec((nf, 1, cols), lo),
            pl.BlockSpec((nf, 1, cols), hi),
            pl.BlockSpec((nf, cols, d), lo, pipeline_mode=once),
        ],
        out_specs=pl.BlockSpec((None, tile, d), row),
        out_shape=jax.ShapeDtypeStruct((bsz, s, d), F32),
        scratch_shapes=[
            pltpu.VMEM((tile, d), BF16),
            pltpu.VMEM((tile + SUBLANES, cols), F32),
            pltpu.VMEM((tile + SUBLANES, cols), F32),
            pltpu.VMEM((tile + SUBLANES, cols), F32),
            pltpu.VMEM((tile + SUBLANES, cols), F32),
            pltpu.VMEM((nf, SUBLANES, cols), F32),
            pltpu.VMEM((nf, SUBLANES, cols), F32),
        ],
        compiler_params=pltpu.CompilerParams(
            dimension_semantics=("parallel", "arbitrary"), vmem_limit_bytes=VMEM_LIMIT_BYTES),
    )(x, ya, yb, yc, yd, wo, ng, w_up_t, w_up_t, cw_t, cw_t, cb_t, cb_t, wd_t)


def _lane_group_mask(n_groups, width, rows=SUBLANES):
    lane = np.arange(n_groups * width) // width
    m = np.zeros((rows, n_groups * width), np.float32)
    for g in range(n_groups):
        m[g] = (lane == g)
    return jnp.asarray(m)


def _block_diag_mask(n, rows_per, cols_per):
    r = np.arange(n * rows_per)[:, None] // rows_per
    c = np.arange(n * cols_per)[None, :] // cols_per
    return jnp.asarray((r == c).astype(np.float32))


def _rope_tables(s, inv_freq, n_heads):
    ang = jnp.arange(s, dtype=F32)[:, None] * inv_freq[None, :]
    cos = jnp.cos(ang)
    sin = jnp.sin(ang)
    cos_t = jnp.tile(jnp.concatenate([cos, cos], axis=-1), (1, n_heads))
    sin_t = jnp.tile(jnp.concatenate([-sin, sin], axis=-1), (1, n_heads))
    return cos_t, sin_t


def kernel(x, attn_norm_g, w_in, moba_q_norm_g, moba_k_norm_g, ssm_conv_w, ssm_conv_b, ssm_dt_bias, ssm_a_log,
           ssm_d, ssm_norm_g, gla_gate_w2, gla_gate_b, gla_norm_g, ret_norm_g, w_out, ffn_norm_g, ffn_w_up,
           ffn_conv_w, ffn_conv_b, ffn_w_down):
    bsz, s, d = x.shape
    depth = w_in.shape[0]
    gw = GROUP_WIDTH

    moba_inv = ROPE_THETA ** (-jnp.arange(0, MOBA_HEAD_DIM, 2, dtype=F32) / MOBA_HEAD_DIM)
    moba_cos, moba_sin = _rope_tables(s, moba_inv, MOBA_HEADS)
    ret_inv = 1.0 / (ROPE_THETA ** jnp.linspace(0.0, 1.0, RET_DK // 2, dtype=F32))
    ret_cos, ret_sin = _rope_tables(s, ret_inv, RET_HEADS)
    head_mean = _block_diag_mask(4, 64, 64) / 64.0
    hmask256 = _lane_group_mask(4, 64)
    qmask128 = _lane_group_mask(4, 32)
    ssm_gmask = _lane_group_mask(SSM_GROUPS, SSM_STATE)
    ssm_smask = _block_diag_mask(SSM_GROUPS, SSM_STATE, 2 * SSM_HEAD_DIM)
    lin_smask = _block_diag_mask(4, 32, 64)
    tri128 = jnp.asarray(np.tril(np.ones((SSM_CHUNK, SSM_CHUNK), np.float32)))
    P = 2 * GLA_CHUNK
    same_chunk = (np.arange(P)[:, None] // GLA_CHUNK) == (np.arange(P)[None, :] // GLA_CHUNK)
    gla_tri = jnp.asarray((np.tril(np.ones((P, P))) * same_chunk).astype(np.float32))
    gla_ones = jnp.asarray(same_chunk.astype(np.float32))

    L = RET_CHUNK
    ret_lg = jnp.log(1.0 - 2.0 ** (-5.0 - jnp.arange(RET_HEADS, dtype=F32)))
    idx = jnp.arange(L, dtype=F32)
    diff = idx[:, None] - idx[None, :]
    ret_dmat = jnp.where(diff >= 0, jnp.exp(jnp.maximum(diff, 0.0)[None] * ret_lg[:, None, None]), 0.0)
    ret_kend = jnp.repeat(jnp.exp((L - 1 - idx)[:, None] * ret_lg[None, :]), RET_DK, axis=1)
    ret_qdec = jnp.repeat(jnp.exp((idx + 1.0)[:, None] * ret_lg[None, :]), RET_DK, axis=1)
    ret_cdec = jnp.repeat(jnp.exp(L * ret_lg), RET_DV)[None, :]

    o_mq, o_mk, o_mv = 0, 256, 512
    o_sz, o_sx, o_sd = 768, 1024, 1536
    o_gq, o_gk, o_gv, o_gr, o_gg = 1540, 1668, 1796, 2052, 2308
    o_rq = 2324

    for l in range(depth):
        wl = w_in[l]
        w_moba = wl[:, o_mq:o_sz].astype(BF16)
        w_ssm = jnp.concatenate(
            [wl[:, o_sz:o_sd], jnp.repeat(wl[:, o_sd:o_sd + SSM_HEADS], SSM_HEAD_DIM, axis=1)], axis=1).astype(BF16)
        w_gla = jnp.concatenate(
            [wl[:, o_gq:o_gg + GLA_GATE_RANK], jnp.zeros((d, LANES - GLA_GATE_RANK), F32)], axis=1).astype(BF16)
        w_ret = wl[:, o_rq:o_rq + 768].astype(BF16)
        ng = attn_norm_g[l][None, :]

        y_moba = _moba_call(x, ng, w_moba, jnp.tile(moba_q_norm_g[l], MOBA_HEADS)[None, :],
                            jnp.tile(moba_k_norm_g[l], MOBA_HEADS)[None, :], moba_cos, moba_sin, head_mean,
                            hmask256)
        y_ssm = _ssd_call(x, ng, w_ssm, ssm_conv_w[l], ssm_conv_b[l][None, :],
                          jnp.repeat(ssm_dt_bias[l], SSM_HEAD_DIM)[None, :],
                          jnp.repeat(ssm_a_log[l], SSM_HEAD_DIM)[None, :],
                          jnp.repeat(ssm_d[l], SSM_HEAD_DIM)[None, :], ssm_norm_g[l][None, :], tri128, ssm_gmask,
                          hmask256, ssm_smask)
        w2 = jnp.concatenate([gla_gate_w2[l], jnp.zeros((LANES - GLA_GATE_RANK, GLA_HEADS * GLA_DK), F32)],
                             axis=0).astype(BF16)
        y_gla = _gla_call(x, ng, w_gla, w2, gla_gate_b[l][None, :], jnp.tile(gla_norm_g[l], GLA_HEADS)[None, :],
                          gla_tri, gla_ones, head_mean, qmask128, hmask256, lin_smask)
        y_ret = _ret_call(x, ng, w_ret, ret_cos, ret_sin, jnp.tile(ret_norm_g[l], RET_HEADS)[None, :], ret_dmat,
                          ret_kend, ret_qdec, ret_cdec, head_mean, qmask128, hmask256, lin_smask)
        x = _ffn_call(x, y_moba, y_ssm, y_gla, y_ret, w_out[l].astype(BF16), ffn_norm_g[l][None, :],
                      ffn_w_up[l].astype(BF16), ffn_conv_w[l], ffn_conv_b[l][None, :], ffn_w_down[l].astype(BF16))
    return x
```

```python
import functools

import numpy as np
import jax
import jax.numpy as jnp
from jax import lax
from jax.experimental import pallas as pl
from jax.experimental.pallas import tpu as pltpu

F32 = jnp.float32
BF16 = jnp.bfloat16
HIGHEST = lax.Precision.HIGHEST

D_MODEL = 1024
GROUP_WIDTH = 256
NORM_EPS = 1e-6
NEG_INF = -1e30

MOBA_HEADS = 4
MOBA_HEAD_DIM = 64
MOBA_BLOCK = 256
MOBA_TOPK = 3
ROPE_THETA = 10000.0

SSM_HEADS = 4
SSM_HEAD_DIM = 64
SSM_GROUPS = 2
SSM_STATE = 64
SSM_CONV = 4
SSM_CHUNK = 128
SSM_CONV_CH = 512

GLA_HEADS = 4
GLA_DV = 64
GLA_DK = 32
GLA_GATE_RANK = 16
GLA_GATE_TAU = 16.0
GLA_CHUNK = 64

RET_HEADS = 4
RET_DV = 64
RET_DK = 32
RET_CHUNK = 128

FFN_DIM = 2816
FFN_CONV = 3

LANES = 128
SUBLANES = 8
VMEM_LIMIT_BYTES = 48 * 1024 * 1024

MIX_TILE = 512
FFN_TILE = 512
FFN_COLS = 256


def _rms_rows(x, g):
    ms = jnp.mean(x * x, axis=-1, keepdims=True)
    return x * lax.rsqrt(ms + NORM_EPS) * g


def _dot(a, b):
    return jnp.dot(a.astype(BF16), b.astype(BF16), preferred_element_type=F32)


def _dot_nt(a, b):
    return lax.dot_general(a.astype(BF16), b.astype(BF16), (((1,), (1,)), ((), ())),
                           preferred_element_type=F32)


def _sigmoid(x):
    return 1.0 / (1.0 + jnp.exp(-x))


def _silu(x):
    return x * _sigmoid(x)


def _softplus(x):
    return jnp.maximum(x, 0.0) + jnp.log1p(jnp.exp(-jnp.abs(x)))


def _split_dot(a, b, passes, split_rhs=False):
    acc = None
    rest = b if split_rhs else a
    for j in range(passes):
        piece = rest.astype(BF16)
        part = jnp.dot(a if split_rhs else piece, piece if split_rhs else b, preferred_element_type=F32)
        acc = part if acc is None else acc + part
        if j + 1 < passes:
            rest = rest - piece.astype(F32)
    return acc


def _rotate_half(t, half):
    slabs = []
    for c in range(t.shape[-1] // LANES):
        u = t[:, c * LANES:(c + 1) * LANES]
        lane = lax.broadcasted_iota(jnp.int32, u.shape, 1)
        first = (lane % (2 * half)) < half
        slabs.append(jnp.where(first, pltpu.roll(u, LANES - half, 1), pltpu.roll(u, half, 1)))
    return slabs[0] if len(slabs) == 1 else jnp.concatenate(slabs, axis=-1)


def _moba_kernel(x_ref, g_ref, wqk_ref, wvt_ref, qg_ref, kg_ref, cos_ref, sin_ref, gm_ref, hmask_ref, y_ref,
                 k_s, vt_s, km_s, qm_s, bias_s, s_s, *, n_blk):
    i = pl.program_id(1)
    blk = MOBA_BLOCK
    gw = GROUP_WIDTH
    dh = MOBA_HEAD_DIM

    h = _rms_rows(x_ref[...], g_ref[...]).astype(BF16)
    p = _dot_nt(h, wqk_ref[...])
    vt = _dot_nt(wvt_ref[...], h)
    gm = gm_ref[...]
    cos = cos_ref[...]
    sin = sin_ref[...]

    def prep(t, gain):
        ms = _split_dot(t * t, gm, 2)
        t = t * lax.rsqrt(ms + NORM_EPS) * gain
        return t * cos + _rotate_half(t, dh // 2) * sin

    q = prep(p[:, 0:gw], qg_ref[...])
    k = prep(p[:, gw:2 * gw], kg_ref[...])

    @pl.when(i == 0)
    def _():
        km_s[...] = jnp.zeros_like(km_s)

    k_s[i] = k.astype(BF16)
    vt_s[i] = vt.astype(BF16)
    km_s[i] = jnp.broadcast_to(jnp.mean(k, axis=0, keepdims=True), (SUBLANES, gw))

    row8 = lax.broadcasted_iota(jnp.int32, (SUBLANES, gw), 0)
    kmat = jnp.zeros((SUBLANES, gw), F32)
    for n in range(n_blk):
        kmat = jnp.where(row8 == n, km_s[n], kmat)
    hmask = hmask_ref[...]
    kmh = jnp.concatenate([kmat * hmask[hd:hd + 1, :] for hd in range(MOBA_HEADS)], axis=0)
    gate = lax.dot_general(kmh, q, (((1,), (1,)), ((), ())), precision=HIGHEST,
                           preferred_element_type=F32)

    rowq = lax.broadcasted_iota(jnp.int32, (SUBLANES, blk), 0)
    valid = rowq < i
    bias_tiles = [jnp.zeros((SUBLANES, blk), F32) for _ in range(n_blk)]
    for hd in range(MOBA_HEADS):
        gh = gate[hd * SUBLANES:(hd + 1) * SUBLANES, :]
        for n in range(n_blk):
            gn = gh[n:n + 1, :]
            beats = valid & ((gh > gn) | ((gh == gn) & (rowq < n)))
            cnt = jnp.sum(beats.astype(F32), axis=0, keepdims=True)
            bias_n = jnp.where(cnt < float(MOBA_TOPK), 0.0, NEG_INF)
            bias_tiles[n] = jnp.where(rowq == hd, bias_n, bias_tiles[n])
    for n in range(n_blk):
        bias_s[n] = bias_tiles[n]

    scale = dh ** -0.5
    for hd in range(MOBA_HEADS):
        qm_s[hd] = (q * (scale * hmask[hd:hd + 1, :])).astype(BF16)

    def attend(ii):
        rj = lax.broadcasted_iota(jnp.int32, (blk, blk), 0)
        cq = lax.broadcasted_iota(jnp.int32, (blk, blk), 1)
        causal = rj <= cq
        outs = []
        for hd in range(MOBA_HEADS):
            qh = qm_s[hd]
            m = None
            for n in range(ii + 1):
                s = _dot_nt(k_s[n], qh)
                if n == ii:
                    s = jnp.where(causal, s, NEG_INF)
                else:
                    s = s + bias_s[n, hd:hd + 1, :]
                s_s[hd, n * blk:(n + 1) * blk, :] = s
                mx = jnp.max(s, axis=0, keepdims=True)
                m = mx if m is None else jnp.maximum(m, mx)
            den = jnp.zeros((1, blk), F32)
            acc = jnp.zeros((dh, blk), F32)
            for n in range(ii + 1):
                pr = jnp.exp(s_s[hd, n * blk:(n + 1) * blk, :] - m)
                den = den + jnp.sum(pr, axis=0, keepdims=True)
                acc = acc + jnp.dot(vt_s[n, hd * dh:(hd + 1) * dh, :], pr.astype(BF16),
                                    preferred_element_type=F32)
            outs.append(acc * (1.0 / den))
        y_ref[...] = jnp.concatenate(outs, axis=0).T.astype(y_ref.dtype)

    for ii in range(n_blk):
        pl.when(i == ii)(functools.partial(attend, ii))


def _moba_call(x, norm_g, wqk, wvt, qg, kg, cos_t, sin_t, gm, hmask):
    bsz, s, d = x.shape
    blk = MOBA_BLOCK
    n_blk = s // blk
    gw = GROUP_WIDTH
    const = lambda b, i: (0, 0)
    return pl.pallas_call(
        functools.partial(_moba_kernel, n_blk=n_blk),
        grid=(bsz, n_blk),
        in_specs=[
            pl.BlockSpec((None, blk, d), lambda b, i: (b, i, 0)),
            pl.BlockSpec((1, d), const),
            pl.BlockSpec((2 * gw, d), const),
            pl.BlockSpec((gw, d), const),
            pl.BlockSpec((1, gw), const),
            pl.BlockSpec((1, gw), const),
            pl.BlockSpec((blk, gw), lambda b, i: (i, 0)),
            pl.BlockSpec((blk, gw), lambda b, i: (i, 0)),
            pl.BlockSpec((gw, gw), const),
            pl.BlockSpec((SUBLANES, gw), const),
        ],
        out_specs=pl.BlockSpec((None, blk, gw), lambda b, i: (b, i, 0)),
        out_shape=jax.ShapeDtypeStruct((bsz, s, gw), BF16),
        scratch_shapes=[
            pltpu.VMEM((n_blk, blk, gw), BF16),
            pltpu.VMEM((n_blk, gw, blk), BF16),
            pltpu.VMEM((n_blk, SUBLANES, gw), F32),
            pltpu.VMEM((MOBA_HEADS, blk, gw), BF16),
            pltpu.VMEM((n_blk, SUBLANES, blk), F32),
            pltpu.VMEM((MOBA_HEADS, s, blk), F32),
        ],
        compiler_params=pltpu.CompilerParams(
            dimension_semantics=("parallel", "arbitrary"), vmem_limit_bytes=VMEM_LIMIT_BYTES),
    )(x, norm_g, wqk, wvt, qg, kg, cos_t, sin_t, gm, hmask)


def _ssd_kernel(x_ref, g_ref, w_ref, cw_ref, cb_ref, dtb_ref, alog_ref, dsk_ref, ng_ref, tri_ref,
                gmask_ref, hmask_ref, smask_ref, y_ref, xp_s, st_s, y_s, *, tile):
    t = pl.program_id(1)
    gw = GROUP_WIDTH
    L = SSM_CHUNK
    gn = SSM_GROUPS * SSM_STATE

    @pl.when(t == 0)
    def _():
        xp_s[0:SUBLANES, :] = jnp.zeros((SUBLANES, SSM_CONV_CH), F32)
        st_s[...] = jnp.zeros_like(st_s)

    h = _rms_rows(x_ref[...], g_ref[...]).astype(BF16)
    p = _dot_nt(h, w_ref[...])
    z = p[:, 0:gw]
    xp_s[SUBLANES:SUBLANES + tile, :] = p[:, gw:gw + SSM_CONV_CH]
    cw = cw_ref[...]
    conv = cb_ref[...]
    for j in range(SSM_CONV):
        conv = conv + cw[j:j + 1, :] * xp_s[pl.ds(SUBLANES - (SSM_CONV - 1) + j, tile), :]
    xp_s[0:SUBLANES, :] = xp_s[tile:tile + SUBLANES, :]
    xbc = _silu(conv)
    xs = xbc[:, 0:gw]
    bm = xbc[:, gw:gw + gn]
    cm = xbc[:, gw + gn:gw + 2 * gn]
    dt = _softplus(p[:, gw + SSM_CONV_CH:2 * gw + SSM_CONV_CH] + dtb_ref[...])
    a = dt * (-jnp.exp(alog_ref[...]))
    xdt = xs * dt

    tri = tri_ref[...]
    gmask = gmask_ref[...]
    hmask = hmask_ref[...]
    smask = smask_ref[...]
    rl = lax.broadcasted_iota(jnp.int32, (L, L), 0)
    cs = lax.broadcasted_iota(jnp.int32, (L, L), 1)
    causal = rl >= cs
    rep = SSM_HEADS // SSM_GROUPS
    for c in range(tile // L):
        sl = slice(c * L, (c + 1) * L)
        a_cs = _split_dot(tri, a[sl], 3, split_rhs=True)
        a_cs_t = a_cs.T
        a_last = a_cs[L - 1:L, :]
        cmc = cm[sl]
        bmc = bm[sl]
        xdtc = xdt[sl]
        y = jnp.zeros((L, gw), F32)
        for g in range(SSM_GROUPS):
            scores = _dot_nt(cmc * gmask[g:g + 1, :], bmc)
            for hd in range(g * rep, (g + 1) * rep):
                col = a_cs[:, hd * SSM_HEAD_DIM:hd * SSM_HEAD_DIM + 1]
                row = a_cs_t[hd * SSM_HEAD_DIM:hd * SSM_HEAD_DIM + 1, :]
                decay = jnp.where(causal, jnp.exp(col - row), 0.0)
                y = y + _dot(scores * decay, xdtc * hmask[hd:hd + 1, :])
        state = st_s[...]
        y = y + _dot(cmc, state) * jnp.exp(a_cs)
        new = _dot(bmc.T, xdtc * jnp.exp(a_last - a_cs)) * smask
        st_s[...] = state * jnp.exp(a_last) + new
        y_s[sl, :] = y

    y = (y_s[...] + xs * dsk_ref[...]) * _silu(z)
    ng = ng_ref[...]
    half = gw // SSM_GROUPS
    for g in range(SSM_GROUPS):
        yg = y[:, g * half:(g + 1) * half]
        ms = jnp.mean(yg * yg, axis=-1, keepdims=True)
        y_ref[:, g * half:(g + 1) * half] = (yg * lax.rsqrt(ms + NORM_EPS) * ng[:, g * half:(g + 1) * half]
                                             ).astype(y_ref.dtype)


def _ssd_call(x, norm_g, w, cw, cb, dtb, alog, dsk, ng, tri, gmask, hmask, smask):
    bsz, s, d = x.shape
    tile = min(MIX_TILE, s)
    gw = GROUP_WIDTH
    wcols = w.shape[0]
    gn = SSM_GROUPS * SSM_STATE
    const = lambda b, t: (0, 0)
    return pl.pallas_call(
        functools.partial(_ssd_kernel, tile=tile),
        grid=(bsz, s // tile),
        in_specs=[
            pl.BlockSpec((None, tile, d), lambda b, t: (b, t, 0)),
            pl.BlockSpec((1, d), const),
            pl.BlockSpec((wcols, d), const),
            pl.BlockSpec((SSM_CONV, SSM_CONV_CH), const),
            pl.BlockSpec((1, SSM_CONV_CH), const),
            pl.BlockSpec((1, gw), const),
            pl.BlockSpec((1, gw), const),
            pl.BlockSpec((1, gw), const),
            pl.BlockSpec((1, gw), const),
            pl.BlockSpec((SSM_CHUNK, SSM_CHUNK), const),
            pl.BlockSpec((SUBLANES, gn), const),
            pl.BlockSpec((SUBLANES, gw), const),
            pl.BlockSpec((gn, gw), const),
        ],
        out_specs=pl.BlockSpec((None, tile, gw), lambda b, t: (b, t, 0)),
        out_shape=jax.ShapeDtypeStruct((bsz, s, gw), BF16),
        scratch_shapes=[
            pltpu.VMEM((tile + SUBLANES, SSM_CONV_CH), F32),
            pltpu.VMEM((gn, gw), F32),
            pltpu.VMEM((tile, gw), F32),
        ],
        compiler_params=pltpu.CompilerParams(
            dimension_semantics=("parallel", "arbitrary"), vmem_limit_bytes=VMEM_LIMIT_BYTES),
    )(x, norm_g, w, cw, cb, dtb, alog, dsk, ng, tri, gmask, hmask, smask)


def _gla_kernel(x_ref, g_ref, w_ref, w2_ref, b2_ref, ng_ref, tri_ref, ones_ref, gm_ref, qmask_ref,
                vmask_ref, smask_ref, y_ref, st_s, o_s, *, tile):
    t = pl.program_id(1)
    gw = GROUP_WIDTH
    L = GLA_CHUNK
    P = 2 * L
    qk = GLA_HEADS * GLA_DK

    @pl.when(t == 0)
    def _():
        st_s[...] = jnp.zeros_like(st_s)

    h = _rms_rows(x_ref[...], g_ref[...]).astype(BF16)
    p = _dot_nt(h, w_ref[...])
    q = p[:, 0:qk] * (GLA_DK ** -0.5)
    k = p[:, qk:2 * qk]
    v = p[:, 2 * qk:2 * qk + gw]
    r = p[:, 2 * qk + gw:2 * qk + 2 * gw]
    g_pre = _dot(p[:, 2 * qk + 2 * gw:2 * qk + 2 * gw + LANES], w2_ref[...]) + b2_ref[...]
    lg = (jnp.minimum(g_pre, 0.0) - jnp.log1p(jnp.exp(-jnp.abs(g_pre)))) / GLA_GATE_TAU

    tri = tri_ref[...]
    ones_bd = ones_ref[...]
    qmask = qmask_ref[...]
    vmask = vmask_ref[...]
    smask = smask_ref[...]
    rl = lax.broadcasted_iota(jnp.int32, (P, P), 0)
    cs = lax.broadcasted_iota(jnp.int32, (P, P), 1)
    causal = (rl >= cs) & ((rl // L) == (cs // L))
    lane_p = lax.broadcasted_iota(jnp.int32, (qk, P), 1)
    for c in range(tile // P):
        sl = slice(c * P, (c + 1) * P)
        gc = lg[sl]
        bcs = _split_dot(tri, gc, 3, split_rhs=True)
        b_last = _split_dot(ones_bd, gc, 3, split_rhs=True)
        qd = q[sl] * jnp.exp(bcs)
        ki = k[sl] * jnp.exp(-bcs)
        ke_t = (k[sl] * jnp.exp(b_last - bcs)).T
        dec_t = jnp.exp(b_last).T
        vc = v[sl]
        o = jnp.zeros((P, gw), F32)
        for hd in range(GLA_HEADS):
            att = jnp.where(causal, _dot_nt(qd * qmask[hd:hd + 1, :], ki), 0.0)
            o = o + _dot(att, vc * vmask[hd:hd + 1, :])
        o_s[sl, :] = o
        for j in range(2):
            rows = slice(c * P + j * L, c * P + (j + 1) * L)
            state = st_s[...]
            o_s[rows, :] = o_s[rows, :] + _dot(qd[j * L:(j + 1) * L], state)
            in_chunk = (lane_p // L) == j
            new = _dot(jnp.where(in_chunk, ke_t, 0.0), vc) * smask
            st_s[...] = state * dec_t[:, j * L:j * L + 1] + new

    o = o_s[...]
    ms = _split_dot(o * o, gm_ref[...], 2)
    y_ref[...] = (o * lax.rsqrt(ms + NORM_EPS) * ng_ref[...] * _silu(r)).astype(y_ref.dtype)


def _gla_call(x, norm_g, w, w2, b2, ng, tri, ones_bd, gm, qmask, vmask, smask):
    bsz, s, d = x.shape
    tile = min(MIX_TILE, s)
    gw = GROUP_WIDTH
    qk = GLA_HEADS * GLA_DK
    P = 2 * GLA_CHUNK
    const = lambda b, t: (0, 0)
    return pl.pallas_call(
        functools.partial(_gla_kernel, tile=tile),
        grid=(bsz, s // tile),
        in_specs=[
            pl.BlockSpec((None, tile, d), lambda b, t: (b, t, 0)),
            pl.BlockSpec((1, d), const),
            pl.BlockSpec((w.shape[0], d), const),
            pl.BlockSpec((LANES, qk), const),
            pl.BlockSpec((1, qk), const),
            pl.BlockSpec((1, gw), const),
            pl.BlockSpec((P, P), const),
            pl.BlockSpec((P, P), const),
            pl.BlockSpec((gw, gw), const),
            pl.BlockSpec((SUBLANES, qk), const),
            pl.BlockSpec((SUBLANES, gw), const),
            pl.BlockSpec((qk, gw), const),
        ],
        out_specs=pl.BlockSpec((None, tile, gw), lambda b, t: (b, t, 0)),
        out_shape=jax.ShapeDtypeStruct((bsz, s, gw), BF16),
        scratch_shapes=[
            pltpu.VMEM((qk, gw), F32),
            pltpu.VMEM((tile, gw), F32),
        ],
        compiler_params=pltpu.CompilerParams(
            dimension_semantics=("parallel", "arbitrary"), vmem_limit_bytes=VMEM_LIMIT_BYTES),
    )(x, norm_g, w, w2, b2, ng, tri, ones_bd, gm, qmask, vmask, smask)


def _ret_kernel(x_ref, g_ref, w_ref, cos_ref, sin_ref, ng_ref, dmat_ref, kend_ref, qdec_ref, cdec_ref,
                gm_ref, qmask_ref, vmask_ref, smask_ref, y_ref, st_s, o_s, *, tile):
    t = pl.program_id(1)
    gw = GROUP_WIDTH
    L = RET_CHUNK
    qk = RET_HEADS * RET_DK

    @pl.when(t == 0)
    def _():
        st_s[...] = jnp.zeros_like(st_s)

    h = _rms_rows(x_ref[...], g_ref[...]).astype(BF16)
    p = _dot_nt(h, w_ref[...])
    cos = cos_ref[...]
    sin = sin_ref[...]

    def rope(u):
        return u * cos + _rotate_half(u, RET_DK // 2) * sin

    q = rope(p[:, 0:qk])
    k = rope(p[:, qk:2 * qk]) * (RET_DK ** -0.5)
    v = p[:, 2 * qk:2 * qk + gw]
    gate = p[:, 2 * qk + gw:2 * qk + 2 * gw]

    kend = kend_ref[...]
    qdec = qdec_ref[...]
    cdec = cdec_ref[...]
    qmask = qmask_ref[...]
    vmask = vmask_ref[...]
    smask = smask_ref[...]
    for c in range(tile // L):
        sl = slice(c * L, (c + 1) * L)
        qc = q[sl]
        kc = k[sl]
        vc = v[sl]
        o = jnp.zeros((L, gw), F32)
        for hd in range(RET_HEADS):
            att = _dot_nt(qc * qmask[hd:hd + 1, :], kc) * dmat_ref[hd]
            o = o + _dot(att, vc * vmask[hd:hd + 1, :])
        state = st_s[...]
        o = o + _dot(qc * qdec, state)
        new = _dot((kc * kend).T, vc) * smask
        st_s[...] = state * cdec + new
        o_s[sl, :] = o

    o = o_s[...]
    ms = _split_dot(o * o, gm_ref[...], 2)
    y_ref[...] = (o * lax.rsqrt(ms + NORM_EPS) * ng_ref[...] * _silu(gate)).astype(y_ref.dtype)


def _ret_call(x, norm_g, w, cos_t, sin_t, ng, dmat, kend, qdec, cdec, gm, qmask, vmask, smask):
    bsz, s, d = x.shape
    tile = min(MIX_TILE, s)
    gw = GROUP_WIDTH
    qk = RET_HEADS * RET_DK
    L = RET_CHUNK
    const = lambda b, t: (0, 0)
    return pl.pallas_call(
        functools.partial(_ret_kernel, tile=tile),
        grid=(bsz, s // tile),
        in_specs=[
            pl.BlockSpec((None, tile, d), lambda b, t: (b, t, 0)),
            pl.BlockSpec((1, d), const),
            pl.BlockSpec((w.shape[0], d), const),
            pl.BlockSpec((tile, qk), lambda b, t: (t, 0)),
            pl.BlockSpec((tile, qk), lambda b, t: (t, 0)),
            pl.BlockSpec((1, gw), const),
            pl.BlockSpec((RET_HEADS, L, L), lambda b, t: (0, 0, 0)),
            pl.BlockSpec((L, qk), const),
            pl.BlockSpec((L, qk), const),
            pl.BlockSpec((1, gw), const),
            pl.BlockSpec((gw, gw), const),
            pl.BlockSpec((SUBLANES, qk), const),
            pl.BlockSpec((SUBLANES, gw), const),
            pl.BlockSpec((qk, gw), const),
        ],
        out_specs=pl.BlockSpec((None, tile, gw), lambda b, t: (b, t, 0)),
        out_shape=jax.ShapeDtypeStruct((bsz, s, gw), BF16),
        scratch_shapes=[
            pltpu.VMEM((qk, gw), F32),
            pltpu.VMEM((tile, gw), F32),
        ],
        compiler_params=pltpu.CompilerParams(
            dimension_semantics=("parallel", "arbitrary"), vmem_limit_bytes=VMEM_LIMIT_BYTES),
    )(x, norm_g, w, cos_t, sin_t, ng, dmat, kend, qdec, cdec, gm, qmask, vmask, smask)


def _ffn_kernel(x_ref, ya_ref, yb_ref, yc_ref, yd_ref, wo_ref, ng_ref, wg_ref, wv_ref, cwg_ref, cwv_ref,
                cbg_ref, cbv_ref, wd_ref, o_ref, h_s, ug_a, uv_a, ug_b, uv_b, cg_s, cv_s, *, tile, nf):
    t = pl.program_id(1)
    gw = GROUP_WIDTH
    halo = SUBLANES

    @pl.when(t == 0)
    def _():
        cg_s[...] = jnp.zeros_like(cg_s)
        cv_s[...] = jnp.zeros_like(cv_s)

    x1 = x_ref[...]
    for j, y_ref in enumerate((ya_ref, yb_ref, yc_ref, yd_ref)):
        x1 = x1 + jnp.dot(y_ref[...], wo_ref[j * gw:(j + 1) * gw, :], preferred_element_type=F32)
    o_ref[...] = x1
    h_s[...] = _rms_rows(x1, ng_ref[...]).astype(BF16)

    def up(f, ug_s, uv_s):
        h = h_s[...]
        for w_ref, u_s, c_s in ((wg_ref, ug_s, cg_s), (wv_ref, uv_s, cv_s)):
            u_s[halo:halo + tile, :] = jnp.dot(h, w_ref[f], preferred_element_type=F32)
            u_s[0:halo, :] = c_s[f]
            c_s[f] = u_s[tile:tile + halo, :]

    def down(f, ug_s, uv_s):
        def conv(u_s, cw_ref, cb_ref):
            cw = cw_ref[f]
            y = cb_ref[f] + cw[FFN_CONV - 1:FFN_CONV, :] * u_s[halo:halo + tile, :]
            for j in range(FFN_CONV - 1):
                y = y + cw[j:j + 1, :] * u_s[pl.ds(halo - (FFN_CONV - 1) + j, tile), :]
            return y

        act = (_silu(conv(ug_s, cwg_ref, cbg_ref)) * conv(uv_s, cwv_ref, cbv_ref)).astype(BF16)
        o_ref[...] += jnp.dot(act, wd_ref[f], preferred_element_type=F32)

    up(0, ug_a, uv_a)

    def pair(j, carry):
        f = 2 * j
        up(f + 1, ug_b, uv_b)
        down(f, ug_a, uv_a)
        up(f + 2, ug_a, uv_a)
        down(f + 1, ug_b, uv_b)
        return carry

    lax.fori_loop(0, (nf - 1) // 2, pair, 0)
    down(nf - 1, ug_a, uv_a)


def _ffn_call(x, ya, yb, yc, yd, wo, ng, w_up, conv_w, conv_b, w_down):
    bsz, s, d = x.shape
    tile = min(FFN_TILE, s)
    cols = FFN_COLS
    nf = FFN_DIM // cols
    assert nf % 2 == 1 and nf * cols == FFN_DIM
    gw = GROUP_WIDTH
    w_up_t = w_up.reshape(d, 2 * nf, cols).transpose(1, 0, 2)
    cw_t = conv_w.reshape(FFN_CONV, 2 * nf, cols).transpose(1, 0, 2)
    cb_t = conv_b.reshape(2 * nf, 1, cols)
    wd_t = w_down.reshape(nf, cols, d)
    row = lambda b, t: (b, t, 0)
    const2 = lambda b, t: (0, 0)
    lo = lambda b, t: (0, 0, 0)
    hi = lambda b, t: (1, 0, 0)
    once = pl.Buffered(1)
    return pl.pallas_call(
        functools.partial(_ffn_kernel, tile=tile, nf=nf),
        grid=(bsz, s // tile),
        in_specs=[
            pl.BlockSpec((None, tile, d), row),
            pl.BlockSpec((None, tile, gw), row),
            pl.BlockSpec((None, tile, gw), row),
            pl.BlockSpec((None, tile, gw), row),
            pl.BlockSpec((None, tile, gw), row),
            pl.BlockSpec((d, d), const2, pipeline_mode=once),
            pl.BlockSpec((1, d), const2),
            pl.BlockSpec((nf, d, cols), lo, pipeline_mode=once),
            pl.BlockSpec((nf, d, cols), hi, pipeline_mode=once),
            pl.BlockSpec((nf, FFN_CONV, cols), lo),
            pl.BlockSpec((nf, FFN_CONV, cols), hi),
            pl.BlockSpec((nf, 1, cols), lo),
            pl.BlockSpec((nf, 1, cols), hi),
            pl.BlockSpec((nf, cols, d), lo, pipeline_mode=once),
        ],
        out_specs=pl.BlockSpec((None, tile, d), row),
        out_shape=jax.ShapeDtypeStruct((bsz, s, d), F32),
        scratch_shapes=[
            pltpu.VMEM((tile, d), BF16),
            pltpu.VMEM((tile + SUBLANES, cols), F32),
            pltpu.VMEM((tile + SUBLANES, cols), F32),
            pltpu.VMEM((tile + SUBLANES, cols), F32),
            pltpu.VMEM((tile + SUBLANES, cols), F32),
            pltpu.VMEM((nf, SUBLANES, cols), F32),
            pltpu.VMEM((nf, SUBLANES, cols), F32),
        ],
        compiler_params=pltpu.CompilerParams(
            dimension_semantics=("parallel", "arbitrary"), vmem_limit_bytes=VMEM_LIMIT_BYTES),
    )(x, ya, yb, yc, yd, wo, ng, w_up_t, w_up_t, cw_t, cw_t, cb_t, cb_t, wd_t)


def _lane_group_mask(n_groups, width, rows=SUBLANES):
    lane = np.arange(n_groups * width) // width
    m = np.zeros((rows, n_groups * width), np.float32)
    for g in range(n_groups):
        m[g] = (lane == g)
    return jnp.asarray(m)


def _block_diag_mask(n, rows_per, cols_per):
    r = np.arange(n * rows_per)[:, None] // rows_per
    c = np.arange(n * cols_per)[None, :] // cols_per
    return jnp.asarray((r == c).astype(np.float32))


def _rope_tables(s, inv_freq, n_heads):
    ang = jnp.arange(s, dtype=F32)[:, None] * inv_freq[None, :]
    cos = jnp.cos(ang)
    sin = jnp.sin(ang)
    cos_t = jnp.tile(jnp.concatenate([cos, cos], axis=-1), (1, n_heads))
    sin_t = jnp.tile(jnp.concatenate([-sin, sin], axis=-1), (1, n_heads))
    return cos_t, sin_t


def kernel(x, attn_norm_g, w_in, moba_q_norm_g, moba_k_norm_g, ssm_conv_w, ssm_conv_b, ssm_dt_bias, ssm_a_log,
           ssm_d, ssm_norm_g, gla_gate_w2, gla_gate_b, gla_norm_g, ret_norm_g, w_out, ffn_norm_g, ffn_w_up,
           ffn_conv_w, ffn_conv_b, ffn_w_down):
    bsz, s, d = x.shape
    depth = w_in.shape[0]
    gw = GROUP_WIDTH

    moba_inv = ROPE_THETA ** (-jnp.arange(0, MOBA_HEAD_DIM, 2, dtype=F32) / MOBA_HEAD_DIM)
    moba_cos, moba_sin = _rope_tables(s, moba_inv, MOBA_HEADS)
    ret_inv = 1.0 / (ROPE_THETA ** jnp.linspace(0.0, 1.0, RET_DK // 2, dtype=F32))
    ret_cos, ret_sin = _rope_tables(s, ret_inv, RET_HEADS)
    head_mean = (_block_diag_mask(4, 64, 64) / 64.0).astype(BF16)
    hmask256 = _lane_group_mask(4, 64)
    qmask128 = _lane_group_mask(4, 32)
    ssm_gmask = _lane_group_mask(SSM_GROUPS, SSM_STATE)
    ssm_smask = _block_diag_mask(SSM_GROUPS, SSM_STATE, 2 * SSM_HEAD_DIM)
    lin_smask = _block_diag_mask(4, 32, 64)
    tri128 = jnp.asarray(np.tril(np.ones((SSM_CHUNK, SSM_CHUNK), np.float32))).astype(BF16)
    P = 2 * GLA_CHUNK
    same_chunk = (np.arange(P)[:, None] // GLA_CHUNK) == (np.arange(P)[None, :] // GLA_CHUNK)
    gla_tri = jnp.asarray((np.tril(np.ones((P, P))) * same_chunk).astype(np.float32)).astype(BF16)
    gla_ones = jnp.asarray(same_chunk.astype(np.float32)).astype(BF16)

    L = RET_CHUNK
    ret_lg = jnp.log(1.0 - 2.0 ** (-5.0 - jnp.arange(RET_HEADS, dtype=F32)))
    idx = jnp.arange(L, dtype=F32)
    diff = idx[:, None] - idx[None, :]
    ret_dmat = jnp.where(diff >= 0, jnp.exp(jnp.maximum(diff, 0.0)[None] * ret_lg[:, None, None]), 0.0)
    ret_kend = jnp.repeat(jnp.exp((L - 1 - idx)[:, None] * ret_lg[None, :]), RET_DK, axis=1)
    ret_qdec = jnp.repeat(jnp.exp((idx + 1.0)[:, None] * ret_lg[None, :]), RET_DK, axis=1)
    ret_cdec = jnp.repeat(jnp.exp(L * ret_lg), RET_DV)[None, :]

    o_mq, o_mk, o_mv = 0, 256, 512
    o_sz, o_sx, o_sd = 768, 1024, 1536
    o_gq, o_gk, o_gv, o_gr, o_gg = 1540, 1668, 1796, 2052, 2308
    o_rq = 2324

    w_in_t = jnp.transpose(w_in, (0, 2, 1))
    for l in range(depth):
        wl = w_in_t[l]
        w_moba_qk = wl[o_mq:o_mv].astype(BF16)
        w_moba_vt = wl[o_mv:o_sz].astype(BF16)
        w_ssm = jnp.concatenate(
            [wl[o_sz:o_sd], jnp.repeat(wl[o_sd:o_sd + SSM_HEADS], SSM_HEAD_DIM, axis=0)], axis=0).astype(BF16)
        w_gla = jnp.concatenate(
            [wl[o_gq:o_gg + GLA_GATE_RANK], jnp.zeros((LANES - GLA_GATE_RANK, d), F32)], axis=0).astype(BF16)
        w_ret = wl[o_rq:o_rq + 768].astype(BF16)
        ng = attn_norm_g[l][None, :]

        y_moba = _moba_call(x, ng, w_moba_qk, w_moba_vt, jnp.tile(moba_q_norm_g[l], MOBA_HEADS)[None, :],
                            jnp.tile(moba_k_norm_g[l], MOBA_HEADS)[None, :], moba_cos, moba_sin, head_mean,
                            hmask256)
        y_ssm = _ssd_call(x, ng, w_ssm, ssm_conv_w[l], ssm_conv_b[l][None, :],
                          jnp.repeat(ssm_dt_bias[l], SSM_HEAD_DIM)[None, :],
                          jnp.repeat(ssm_a_log[l], SSM_HEAD_DIM)[None, :],
                          jnp.repeat(ssm_d[l], SSM_HEAD_DIM)[None, :], ssm_norm_g[l][None, :], tri128, ssm_gmask,
                          hmask256, ssm_smask)
        w2 = jnp.concatenate([gla_gate_w2[l], jnp.zeros((LANES - GLA_GATE_RANK, GLA_HEADS * GLA_DK), F32)],
                             axis=0).astype(BF16)
        y_gla = _gla_call(x, ng, w_gla, w2, gla_gate_b[l][None, :], jnp.tile(gla_norm_g[l], GLA_HEADS)[None, :],
                          gla_tri, gla_ones, head_mean, qmask128, hmask256, lin_smask)
        y_ret = _ret_call(x, ng, w_ret, ret_cos, ret_sin, jnp.tile(ret_norm_g[l], RET_HEADS)[None, :], ret_dmat,
                          ret_kend, ret_qdec, ret_cdec, head_mean, qmask128, hmask256, lin_smask)
        x = _ffn_call(x, y_moba, y_ssm, y_gla, y_ret, w_out[l].astype(BF16), ffn_norm_g[l][None, :],
                      ffn_w_up[l].astype(BF16), ffn_conv_w[l], ffn_conv_b[l][None, :], ffn_w_down[l].astype(BF16))
    return x
```

```python
import functools

import numpy as np
import jax
import jax.numpy as jnp
from jax import lax
from jax.experimental import pallas as pl
from jax.experimental.pallas import tpu as pltpu

F32 = jnp.float32
BF16 = jnp.bfloat16
HIGHEST = lax.Precision.HIGHEST

D_MODEL = 1024
GROUP_WIDTH = 256
NORM_EPS = 1e-6
NEG_INF = -1e30

MOBA_HEADS = 4
MOBA_HEAD_DIM = 64
MOBA_BLOCK = 256
MOBA_TOPK = 3
ROPE_THETA = 10000.0

SSM_HEADS = 4
SSM_HEAD_DIM = 64
SSM_GROUPS = 2
SSM_STATE = 64
SSM_CONV = 4
SSM_CHUNK = 128
SSM_CONV_CH = 512

GLA_HEADS = 4
GLA_DV = 64
GLA_DK = 32
GLA_GATE_RANK = 16
GLA_GATE_TAU = 16.0
GLA_CHUNK = 64

RET_HEADS = 4
RET_DV = 64
RET_DK = 32
RET_CHUNK = 128

FFN_DIM = 2816
FFN_CONV = 3

LANES = 128
SUBLANES = 8
VMEM_LIMIT_BYTES = 48 * 1024 * 1024

MIX_ROWS = 2
SSD_ROWS = 1
MIX_TILE = 512
FFN_TILE = 512
FFN_COLS = 256


def _rms_rows(x, g):
    ms = jnp.mean(x * x, axis=-1, keepdims=True)
    return x * lax.rsqrt(ms + NORM_EPS) * g


def _dot(a, b):
    return jnp.dot(a.astype(BF16), b.astype(BF16), preferred_element_type=F32)


def _dot_nt(a, b):
    return lax.dot_general(a.astype(BF16), b.astype(BF16), (((1,), (1,)), ((), ())),
                           preferred_element_type=F32)


def _silu(x):
    half = 0.5 * x
    return half + half * jnp.tanh(half)


def _softplus(x):
    return jnp.maximum(x, 0.0) + jnp.log(1.0 + jnp.exp(-jnp.abs(x)))


def _split_dot(a, b, passes, split_rhs=False):
    acc = None
    rest = b if split_rhs else a
    for j in range(passes):
        piece = rest.astype(BF16)
        part = jnp.dot(a if split_rhs else piece, piece if split_rhs else b, preferred_element_type=F32)
        acc = part if acc is None else acc + part
        if j + 1 < passes:
            rest = rest - piece.astype(F32)
    return acc


def _rotate_half(t, half):
    slabs = []
    for c in range(t.shape[-1] // LANES):
        u = t[:, c * LANES:(c + 1) * LANES]
        lane = lax.broadcasted_iota(jnp.int32, u.shape, 1)
        first = (lane % (2 * half)) < half
        slabs.append(jnp.where(first, pltpu.roll(u, LANES - half, 1), pltpu.roll(u, half, 1)))
    return slabs[0] if len(slabs) == 1 else jnp.concatenate(slabs, axis=-1)


def _moba_prep(i, x_ref, g_ref, wqk_ref, wvt_ref, qg_ref, kg_ref, cos_ref, sin_ref, gm_ref, hmask_ref,
               k_s, vt_s, km_s, qm_s, bias_s, *, n_blk):
    blk = MOBA_BLOCK
    gw = GROUP_WIDTH
    dh = MOBA_HEAD_DIM

    h = _rms_rows(x_ref[...], g_ref[...]).astype(BF16)
    p = _dot_nt(h, wqk_ref[...])
    vt = _dot_nt(wvt_ref[...], h)
    gm = gm_ref[...]
    cos = cos_ref[...]
    sin = sin_ref[...]

    def prep(t, gain):
        ms = _split_dot(t * t, gm, 2)
        t = t * lax.rsqrt(ms + NORM_EPS) * gain
        return t * cos + _rotate_half(t, dh // 2) * sin

    q = prep(p[:, 0:gw], qg_ref[...])
    k = prep(p[:, gw:2 * gw], kg_ref[...])

    k_s[i] = k.astype(BF16)
    vt_s[i] = vt.astype(BF16)
    km_s[i] = jnp.broadcast_to(jnp.mean(k, axis=0, keepdims=True), (SUBLANES, gw))

    row8 = lax.broadcasted_iota(jnp.int32, (SUBLANES, gw), 0)
    kmat = jnp.zeros((SUBLANES, gw), F32)
    for n in range(n_blk):
        kmat = jnp.where(row8 == n, km_s[n], kmat)
    hmask = hmask_ref[...]
    kmh = jnp.concatenate([kmat * hmask[hd:hd + 1, :] for hd in range(MOBA_HEADS)], axis=0)
    gate = lax.dot_general(kmh, q, (((1,), (1,)), ((), ())), precision=HIGHEST,
                           preferred_element_type=F32)

    rowq = lax.broadcasted_iota(jnp.int32, (SUBLANES, blk), 0)
    valid = rowq < i
    bias_tiles = [jnp.zeros((SUBLANES, blk), F32) for _ in range(n_blk)]
    for hd in range(MOBA_HEADS):
        gh = gate[hd * SUBLANES:(hd + 1) * SUBLANES, :]
        for n in range(n_blk):
            gn = gh[n:n + 1, :]
            beats = valid & ((gh > gn) | ((gh == gn) & (rowq < n)))
            cnt = jnp.sum(beats.astype(F32), axis=0, keepdims=True)
            bias_n = jnp.where(cnt < float(MOBA_TOPK), 0.0, NEG_INF)
            bias_tiles[n] = jnp.where(rowq == hd, bias_n, bias_tiles[n])
    for n in range(n_blk):
        bias_s[n] = bias_tiles[n]

    scale = dh ** -0.5
    for hd in range(MOBA_HEADS):
        qm_s[hd] = (q * (scale * hmask[hd:hd + 1, :])).astype(BF16)


def _moba_attend(ii, y_ref, k_s, vt_s, qm_s, bias_s, s_s):
    blk = MOBA_BLOCK
    dh = MOBA_HEAD_DIM
    rj = lax.broadcasted_iota(jnp.int32, (blk, blk), 0)
    cq = lax.broadcasted_iota(jnp.int32, (blk, blk), 1)
    causal = rj <= cq
    outs = []
    for hd in range(MOBA_HEADS):
        qh = qm_s[hd]
        m = None
        for n in range(ii + 1):
            s = _dot_nt(k_s[n], qh)
            if n == ii:
                s = jnp.where(causal, s, NEG_INF)
            else:
                s = s + bias_s[n, hd:hd + 1, :]
            s_s[hd, n * blk:(n + 1) * blk, :] = s
            mx = jnp.max(s, axis=0, keepdims=True)
            m = mx if m is None else jnp.maximum(m, mx)
        den = jnp.zeros((1, blk), F32)
        acc = jnp.zeros((dh, blk), F32)
        for n in range(ii + 1):
            pr = jnp.exp(s_s[hd, n * blk:(n + 1) * blk, :] - m)
            den = den + jnp.sum(pr, axis=0, keepdims=True)
            acc = acc + jnp.dot(vt_s[n, hd * dh:(hd + 1) * dh, :], pr.astype(BF16),
                                preferred_element_type=F32)
        outs.append(acc * (1.0 / den))
    y_ref[...] = jnp.concatenate(outs, axis=0).T.astype(y_ref.dtype)


def _moba_kernel(*refs, n_blk, rows):
    i = pl.program_id(1)
    x_ref, shared, y_ref = refs[0], refs[1:10], refs[10]
    k_s, vt_s, km_s, qm_s, bias_s, s_s = refs[11:]

    @pl.when(i == 0)
    def _():
        km_s[...] = jnp.zeros_like(km_s)

    for r in range(rows):
        _moba_prep(i, x_ref.at[r], *shared, k_s.at[r], vt_s.at[r], km_s.at[r], qm_s.at[r], bias_s.at[r],
                   n_blk=n_blk)

    def attend(ii):
        for r in range(rows):
            _moba_attend(ii, y_ref.at[r], k_s.at[r], vt_s.at[r], qm_s.at[r], bias_s.at[r], s_s.at[r])

    for ii in range(n_blk):
        pl.when(i == ii)(functools.partial(attend, ii))


def _moba_call(x, norm_g, wqk, wvt, qg, kg, cos_t, sin_t, gm, hmask):
    bsz, s, d = x.shape
    blk = MOBA_BLOCK
    n_blk = s // blk
    gw = GROUP_WIDTH
    rows = MIX_ROWS
    assert bsz % rows == 0
    const = lambda b, i: (0, 0)
    return pl.pallas_call(
        functools.partial(_moba_kernel, n_blk=n_blk, rows=rows),
        grid=(bsz // rows, n_blk),
        in_specs=[
            pl.BlockSpec((rows, blk, d), lambda b, i: (b, i, 0)),
            pl.BlockSpec((1, d), const),
            pl.BlockSpec((2 * gw, d), const),
            pl.BlockSpec((gw, d), const),
            pl.BlockSpec((1, gw), const),
            pl.BlockSpec((1, gw), const),
            pl.BlockSpec((blk, gw), lambda b, i: (i, 0)),
            pl.BlockSpec((blk, gw), lambda b, i: (i, 0)),
            pl.BlockSpec((gw, gw), const),
            pl.BlockSpec((SUBLANES, gw), const),
        ],
        out_specs=pl.BlockSpec((rows, blk, gw), lambda b, i: (b, i, 0)),
        out_shape=jax.ShapeDtypeStruct((bsz, s, gw), BF16),
        scratch_shapes=[
            pltpu.VMEM((rows, n_blk, blk, gw), BF16),
            pltpu.VMEM((rows, n_blk, gw, blk), BF16),
            pltpu.VMEM((rows, n_blk, SUBLANES, gw), F32),
            pltpu.VMEM((rows, MOBA_HEADS, blk, gw), BF16),
            pltpu.VMEM((rows, n_blk, SUBLANES, blk), F32),
            pltpu.VMEM((rows, MOBA_HEADS, s, blk), F32),
        ],
        compiler_params=pltpu.CompilerParams(
            dimension_semantics=("parallel", "arbitrary"), vmem_limit_bytes=VMEM_LIMIT_BYTES),
    )(x, norm_g, wqk, wvt, qg, kg, cos_t, sin_t, gm, hmask)


def _rows_kernel(row_fn, rows, n_shared):
    def kernel(*refs):
        x_ref, shared, y_ref, scratch = refs[0], refs[1:1 + n_shared], refs[1 + n_shared], refs[2 + n_shared:]

        @pl.when(pl.program_id(1) == 0)
        def _():
            for s_ref in scratch:
                s_ref[...] = jnp.zeros_like(s_ref)

        for r in range(rows):
            row_fn(x_ref.at[r], *shared, y_ref.at[r], *(s_ref.at[r] for s_ref in scratch))

    return kernel


def _ssd_row(x_ref, g_ref, w_ref, cw_ref, cb_ref, dtb_ref, alog_ref, dsk_ref, ng_ref, tri_ref,
             gmask_ref, hmask_ref, smask_ref, y_ref, xp_s, st_s, y_s, *, tile):
    gw = GROUP_WIDTH
    L = SSM_CHUNK
    gn = SSM_GROUPS * SSM_STATE

    h = _rms_rows(x_ref[...], g_ref[...]).astype(BF16)
    p = _dot_nt(h, w_ref[...])
    z = p[:, 0:gw]
    xp_s[SUBLANES:SUBLANES + tile, :] = p[:, gw:gw + SSM_CONV_CH]
    cw = cw_ref[...]
    conv = cb_ref[...]
    for j in range(SSM_CONV):
        conv = conv + cw[j:j + 1, :] * xp_s[pl.ds(SUBLANES - (SSM_CONV - 1) + j, tile), :]
    xp_s[0:SUBLANES, :] = xp_s[tile:tile + SUBLANES, :]
    xbc = _silu(conv)
    xs = xbc[:, 0:gw]
    bm = xbc[:, gw:gw + gn]
    cm = xbc[:, gw + gn:gw + 2 * gn]
    dt = _softplus(p[:, gw + SSM_CONV_CH:2 * gw + SSM_CONV_CH] + dtb_ref[...])
    a = dt * (-jnp.exp(alog_ref[...]))
    xdt = xs * dt

    tri = tri_ref[...]
    gmask = gmask_ref[...]
    hmask = hmask_ref[...]
    smask = smask_ref[...]
    rl = lax.broadcasted_iota(jnp.int32, (L, L), 0)
    cs = lax.broadcasted_iota(jnp.int32, (L, L), 1)
    causal = rl >= cs
    rep = SSM_HEADS // SSM_GROUPS
    for c in range(tile // L):
        sl = slice(c * L, (c + 1) * L)
        a_cs = _split_dot(tri, a[sl], 3, split_rhs=True)
        a_cs_t = a_cs.T
        a_last = a_cs[L - 1:L, :]
        cmc = cm[sl]
        bmc = bm[sl]
        xdtc = xdt[sl]
        scores = _dot_nt(jnp.concatenate([cmc * gmask[g:g + 1, :] for g in range(SSM_GROUPS)], axis=0), bmc)
        weighted = []
        for hd in range(SSM_HEADS):
            g = hd // rep
            col = a_cs[:, hd * SSM_HEAD_DIM:hd * SSM_HEAD_DIM + 1]
            row = a_cs_t[hd * SSM_HEAD_DIM:hd * SSM_HEAD_DIM + 1, :]
            decay = jnp.where(causal, jnp.exp(col - row), 0.0)
            weighted.append(scores[g * L:(g + 1) * L, :] * decay)
        y_heads = _dot(jnp.concatenate(weighted, axis=0), xdtc)
        state = st_s[...]
        y = _dot(cmc, state) * jnp.exp(a_cs)
        for hd in range(SSM_HEADS):
            y = y + y_heads[hd * L:(hd + 1) * L, :] * hmask[hd:hd + 1, :]
        new = _dot(bmc.T, xdtc * jnp.exp(a_last - a_cs)) * smask
        st_s[...] = state * jnp.exp(a_last) + new
        y_s[sl, :] = y

    y = (y_s[...] + xs * dsk_ref[...]) * _silu(z)
    ng = ng_ref[...]
    half = gw // SSM_GROUPS
    for g in range(SSM_GROUPS):
        yg = y[:, g * half:(g + 1) * half]
        ms = jnp.mean(yg * yg, axis=-1, keepdims=True)
        y_ref[:, g * half:(g + 1) * half] = (yg * lax.rsqrt(ms + NORM_EPS) * ng[:, g * half:(g + 1) * half]
                                             ).astype(y_ref.dtype)


def _ssd_call(x, norm_g, w, cw, cb, dtb, alog, dsk, ng, tri, gmask, hmask, smask):
    bsz, s, d = x.shape
    tile = min(MIX_TILE, s)
    gw = GROUP_WIDTH
    wcols = w.shape[0]
    gn = SSM_GROUPS * SSM_STATE
    const = lambda b, t: (0, 0)
    rows = SSD_ROWS
    assert bsz % rows == 0
    return pl.pallas_call(
        _rows_kernel(functools.partial(_ssd_row, tile=tile), rows, 12),
        grid=(bsz // rows, s // tile),
        in_specs=[
            pl.BlockSpec((rows, tile, d), lambda b, t: (b, t, 0)),
            pl.BlockSpec((1, d), const),
            pl.BlockSpec((wcols, d), const),
            pl.BlockSpec((SSM_CONV, SSM_CONV_CH), const),
            pl.BlockSpec((1, SSM_CONV_CH), const),
            pl.BlockSpec((1, gw), const),
            pl.BlockSpec((1, gw), const),
            pl.BlockSpec((1, gw), const),
            pl.BlockSpec((1, gw), const),
            pl.BlockSpec((SSM_CHUNK, SSM_CHUNK), const),
            pl.BlockSpec((SUBLANES, gn), const),
            pl.BlockSpec((SUBLANES, gw), const),
            pl.BlockSpec((gn, gw), const),
        ],
        out_specs=pl.BlockSpec((rows, tile, gw), lambda b, t: (b, t, 0)),
        out_shape=jax.ShapeDtypeStruct((bsz, s, gw), BF16),
        scratch_shapes=[
            pltpu.VMEM((rows, tile + SUBLANES, SSM_CONV_CH), F32),
            pltpu.VMEM((rows, gn, gw), F32),
            pltpu.VMEM((rows, tile, gw), F32),
        ],
        compiler_params=pltpu.CompilerParams(
            dimension_semantics=("parallel", "arbitrary"), vmem_limit_bytes=VMEM_LIMIT_BYTES),
    )(x, norm_g, w, cw, cb, dtb, alog, dsk, ng, tri, gmask, hmask, smask)


def _gla_row(x_ref, g_ref, w_ref, w2_ref, b2_ref, ng_ref, tri_ref, gm_ref, qmask_ref,
             vmask_ref, smask_ref, y_ref, st_s, o_s, *, tile):
    gw = GROUP_WIDTH
    L = GLA_CHUNK
    P = 2 * L
    qk = GLA_HEADS * GLA_DK

    h = _rms_rows(x_ref[...], g_ref[...]).astype(BF16)
    p = _dot_nt(h, w_ref[...])
    q = p[:, 0:qk] * (GLA_DK ** -0.5)
    k = p[:, qk:2 * qk]
    v = p[:, 2 * qk:2 * qk + gw]
    r = p[:, 2 * qk + gw:2 * qk + 2 * gw]
    g_pre = _dot(p[:, 2 * qk + 2 * gw:2 * qk + 2 * gw + LANES], w2_ref[...]) + b2_ref[...]
    lg = -_softplus(-g_pre) / GLA_GATE_TAU

    tri = tri_ref[...]
    qmask = qmask_ref[...]
    vmask = vmask_ref[...]
    smask = smask_ref[...]
    rl = lax.broadcasted_iota(jnp.int32, (P, P), 0)
    cs = lax.broadcasted_iota(jnp.int32, (P, P), 1)
    causal = (rl >= cs) & ((rl // L) == (cs // L))
    lane_p = lax.broadcasted_iota(jnp.int32, (qk, P), 1)
    row_p = lax.broadcasted_iota(jnp.int32, (P, qk), 0)
    for c in range(tile // P):
        sl = slice(c * P, (c + 1) * P)
        gc = lg[sl]
        bcs = _split_dot(tri, gc, 3, split_rhs=True)
        b_last = jnp.where(row_p < L, bcs[L - 1:L, :], bcs[P - 1:P, :])
        qd = q[sl] * jnp.exp(bcs)
        ki = k[sl] * jnp.exp(-bcs)
        ke_t = (k[sl] * jnp.exp(b_last - bcs)).T
        dec_t = jnp.exp(b_last).T
        vc = v[sl]
        o = jnp.zeros((P, gw), F32)
        for hd in range(GLA_HEADS):
            att = jnp.where(causal, _dot_nt(qd * qmask[hd:hd + 1, :], ki), 0.0)
            o = o + _dot(att, vc * vmask[hd:hd + 1, :])
        o_s[sl, :] = o
        for j in range(2):
            rows = slice(c * P + j * L, c * P + (j + 1) * L)
            state = st_s[...]
            o_s[rows, :] = o_s[rows, :] + _dot(qd[j * L:(j + 1) * L], state)
            in_chunk = (lane_p // L) == j
            new = _dot(jnp.where(in_chunk, ke_t, 0.0), vc) * smask
            st_s[...] = state * dec_t[:, j * L:j * L + 1] + new

    o = o_s[...]
    ms = _split_dot(o * o, gm_ref[...], 2)
    y_ref[...] = (o * lax.rsqrt(ms + NORM_EPS) * ng_ref[...] * _silu(r)).astype(y_ref.dtype)


def _gla_call(x, norm_g, w, w2, b2, ng, tri, gm, qmask, vmask, smask):
    bsz, s, d = x.shape
    tile = min(MIX_TILE, s)
    gw = GROUP_WIDTH
    qk = GLA_HEADS * GLA_DK
    P = 2 * GLA_CHUNK
    const = lambda b, t: (0, 0)
    rows = MIX_ROWS
    assert bsz % rows == 0
    return pl.pallas_call(
        _rows_kernel(functools.partial(_gla_row, tile=tile), rows, 10),
        grid=(bsz // rows, s // tile),
        in_specs=[
            pl.BlockSpec((rows, tile, d), lambda b, t: (b, t, 0)),
            pl.BlockSpec((1, d), const),
            pl.BlockSpec((w.shape[0], d), const),
            pl.BlockSpec((LANES, qk), const),
            pl.BlockSpec((1, qk), const),
            pl.BlockSpec((1, gw), const),
            pl.BlockSpec((P, P), const),
            pl.BlockSpec((gw, gw), const),
            pl.BlockSpec((SUBLANES, qk), const),
            pl.BlockSpec((SUBLANES, gw), const),
            pl.BlockSpec((qk, gw), const),
        ],
        out_specs=pl.BlockSpec((rows, tile, gw), lambda b, t: (b, t, 0)),
        out_shape=jax.ShapeDtypeStruct((bsz, s, gw), BF16),
        scratch_shapes=[
            pltpu.VMEM((rows, qk, gw), F32),
            pltpu.VMEM((rows, tile, gw), F32),
        ],
        compiler_params=pltpu.CompilerParams(
            dimension_semantics=("parallel", "arbitrary"), vmem_limit_bytes=VMEM_LIMIT_BYTES),
    )(x, norm_g, w, w2, b2, ng, tri, gm, qmask, vmask, smask)


def _ret_row(x_ref, g_ref, w_ref, cos_ref, sin_ref, ng_ref, dmat_ref, kend_ref, qdec_ref, cdec_ref,
             gm_ref, qmask_ref, vmask_ref, smask_ref, y_ref, st_s, o_s, *, tile):
    gw = GROUP_WIDTH
    L = RET_CHUNK
    qk = RET_HEADS * RET_DK

    h = _rms_rows(x_ref[...], g_ref[...]).astype(BF16)
    p = _dot_nt(h, w_ref[...])
    cos = cos_ref[...]
    sin = sin_ref[...]

    def rope(u):
        return u * cos + _rotate_half(u, RET_DK // 2) * sin

    q = rope(p[:, 0:qk])
    k = rope(p[:, qk:2 * qk]) * (RET_DK ** -0.5)
    v = p[:, 2 * qk:2 * qk + gw]
    gate = p[:, 2 * qk + gw:2 * qk + 2 * gw]

    dmat = dmat_ref[...]
    kend = kend_ref[...]
    qdec = qdec_ref[...]
    cdec = cdec_ref[...]
    qmask = qmask_ref[...]
    vmask = vmask_ref[...]
    smask = smask_ref[...]
    for c in range(tile // L):
        sl = slice(c * L, (c + 1) * L)
        qc = q[sl]
        kc = k[sl]
        vc = v[sl]
        q_heads = jnp.concatenate([qc * qmask[hd:hd + 1, :] for hd in range(RET_HEADS)], axis=0)
        o_heads = _dot(_dot_nt(q_heads, kc) * dmat, vc)
        state = st_s[...]
        o = _dot(qc * qdec, state)
        for hd in range(RET_HEADS):
            o = o + o_heads[hd * L:(hd + 1) * L, :] * vmask[hd:hd + 1, :]
        new = _dot((kc * kend).T, vc) * smask
        st_s[...] = state * cdec + new
        o_s[sl, :] = o

    o = o_s[...]
    ms = _split_dot(o * o, gm_ref[...], 2)
    y_ref[...] = (o * lax.rsqrt(ms + NORM_EPS) * ng_ref[...] * _silu(gate)).astype(y_ref.dtype)


def _ret_call(x, norm_g, w, cos_t, sin_t, ng, dmat, kend, qdec, cdec, gm, qmask, vmask, smask):
    bsz, s, d = x.shape
    tile = min(MIX_TILE, s)
    gw = GROUP_WIDTH
    qk = RET_HEADS * RET_DK
    L = RET_CHUNK
    const = lambda b, t: (0, 0)
    rows = MIX_ROWS
    assert bsz % rows == 0
    return pl.pallas_call(
        _rows_kernel(functools.partial(_ret_row, tile=tile), rows, 13),
        grid=(bsz // rows, s // tile),
        in_specs=[
            pl.BlockSpec((rows, tile, d), lambda b, t: (b, t, 0)),
            pl.BlockSpec((1, d), const),
            pl.BlockSpec((w.shape[0], d), const),
            pl.BlockSpec((tile, qk), lambda b, t: (t, 0)),
            pl.BlockSpec((tile, qk), lambda b, t: (t, 0)),
            pl.BlockSpec((1, gw), const),
            pl.BlockSpec((RET_HEADS * L, L), const),
            pl.BlockSpec((L, qk), const),
            pl.BlockSpec((L, qk), const),
            pl.BlockSpec((1, gw), const),
            pl.BlockSpec((gw, gw), const),
            pl.BlockSpec((SUBLANES, qk), const),
            pl.BlockSpec((SUBLANES, gw), const),
            pl.BlockSpec((qk, gw), const),
        ],
        out_specs=pl.BlockSpec((rows, tile, gw), lambda b, t: (b, t, 0)),
        out_shape=jax.ShapeDtypeStruct((bsz, s, gw), BF16),
        scratch_shapes=[
            pltpu.VMEM((rows, qk, gw), F32),
            pltpu.VMEM((rows, tile, gw), F32),
        ],
        compiler_params=pltpu.CompilerParams(
            dimension_semantics=("parallel", "arbitrary"), vmem_limit_bytes=VMEM_LIMIT_BYTES),
    )(x, norm_g, w, cos_t, sin_t, ng, dmat, kend, qdec, cdec, gm, qmask, vmask, smask)


def _ffn_kernel(x_ref, ya_ref, yb_ref, yc_ref, yd_ref, wo_ref, ng_ref, wg_ref, wv_ref, cwg_ref, cwv_ref,
                cbg_ref, cbv_ref, wd_ref, o_ref, h_s, cg_s, cv_s, *u_bufs, tile, nf):
    t = pl.program_id(1)
    gw = GROUP_WIDTH
    cols = FFN_COLS
    halo = SUBLANES

    @pl.when(t == 0)
    def _():
        cg_s[...] = jnp.zeros_like(cg_s)
        cv_s[...] = jnp.zeros_like(cv_s)

    x1 = x_ref[...]
    for j, y_ref in enumerate((ya_ref, yb_ref, yc_ref, yd_ref)):
        x1 = x1 + jnp.dot(y_ref[...], wo_ref[j * gw:(j + 1) * gw, :], preferred_element_type=F32)
    o_ref[...] = x1
    h_s[...] = _rms_rows(x1, ng_ref[...]).astype(BF16)

    def up(f, ug_s, uv_s):
        h = h_s[...]
        for w_ref, u_s, c_s in ((wg_ref, ug_s, cg_s), (wv_ref, uv_s, cv_s)):
            u_s[halo:halo + tile, :] = jnp.dot(h, w_ref[f], preferred_element_type=F32)
            u_s[0:halo, :] = c_s[f]
            c_s[f] = u_s[tile:tile + halo, :]

    def activation(f, ug_s, uv_s):
        def conv(u_s, cw_ref, cb_ref):
            cw = cw_ref[f]
            y = cb_ref[f] + cw[FFN_CONV - 1:FFN_CONV, :] * u_s[halo:halo + tile, :]
            for j in range(FFN_CONV - 1):
                y = y + cw[j:j + 1, :] * u_s[pl.ds(halo - (FFN_CONV - 1) + j, tile), :]
            return y

        return (_silu(conv(ug_s, cwg_ref, cbg_ref)) * conv(uv_s, cwv_ref, cbv_ref)).astype(BF16)

    def down(f, slot, n):
        act = [activation(f + k, *slot[k]) for k in range(n)]
        act = act[0] if n == 1 else jnp.concatenate(act, axis=1)
        o_ref[...] += jnp.dot(act, wd_ref[f * cols:(f + n) * cols, :], preferred_element_type=F32)

    slots = (((u_bufs[0], u_bufs[1]), (u_bufs[2], u_bufs[3])), ((u_bufs[4], u_bufs[5]), (u_bufs[6], u_bufs[7])))
    groups = [(f, min(2, nf - f)) for f in range(0, nf, 2)]
    for k in range(groups[0][1]):
        up(groups[0][0] + k, *slots[0][k])
    for gi, (f, n) in enumerate(groups):
        if gi + 1 < len(groups):
            f2, n2 = groups[gi + 1]
            for k in range(n2):
                up(f2 + k, *slots[(gi + 1) % 2][k])
        down(f, slots[gi % 2], n)


def _ffn_call(x, ya, yb, yc, yd, wo, ng, w_up, conv_w, conv_b, w_down):
    bsz, s, d = x.shape
    tile = min(FFN_TILE, s)
    cols = FFN_COLS
    nf = FFN_DIM // cols
    assert nf * cols == FFN_DIM
    gw = GROUP_WIDTH
    halo = SUBLANES
    w_up_t = w_up.reshape(d, 2 * nf, cols).transpose(1, 0, 2)
    cw_t = conv_w.reshape(FFN_CONV, 2 * nf, cols).transpose(1, 0, 2)
    cb_t = conv_b.reshape(2 * nf, 1, cols)
    row = lambda b, t: (b, t, 0)
    const2 = lambda b, t: (0, 0)
    lo = lambda b, t: (0, 0, 0)
    hi = lambda b, t: (1, 0, 0)
    once = pl.Buffered(1)
    return pl.pallas_call(
        functools.partial(_ffn_kernel, tile=tile, nf=nf),
        grid=(bsz, s // tile),
        in_specs=[
            pl.BlockSpec((None, tile, d), row),
            pl.BlockSpec((None, tile, gw), row),
            pl.BlockSpec((None, tile, gw), row),
            pl.BlockSpec((None, tile, gw), row),
            pl.BlockSpec((None, tile, gw), row),
            pl.BlockSpec((d, d), const2, pipeline_mode=once),
            pl.BlockSpec((1, d), const2),
            pl.BlockSpec((nf, d, cols), lo, pipeline_mode=once),
            pl.BlockSpec((nf, d, cols), hi, pipeline_mode=once),
            pl.BlockSpec((nf, FFN_CONV, cols), lo),
            pl.BlockSpec((nf, FFN_CONV, cols), hi),
            pl.BlockSpec((nf, 1, cols), lo),
            pl.BlockSpec((nf, 1, cols), hi),
            pl.BlockSpec((FFN_DIM, d), const2, pipeline_mode=once),
        ],
        out_specs=pl.BlockSpec((None, tile, d), row),
        out_shape=jax.ShapeDtypeStruct((bsz, s, d), F32),
        scratch_shapes=[
            pltpu.VMEM((tile, d), BF16),
            pltpu.VMEM((nf, halo, cols), F32),
            pltpu.VMEM((nf, halo, cols), F32),
        ] + [pltpu.VMEM((tile + halo, cols), F32)] * 8,
        compiler_params=pltpu.CompilerParams(
            dimension_semantics=("parallel", "arbitrary"), vmem_limit_bytes=VMEM_LIMIT_BYTES),
    )(x, ya, yb, yc, yd, wo, ng, w_up_t, w_up_t, cw_t, cw_t, cb_t, cb_t, w_down)


def _lane_group_mask(n_groups, width, rows=SUBLANES):
    lane = np.arange(n_groups * width) // width
    m = np.zeros((rows, n_groups * width), np.float32)
    for g in range(n_groups):
        m[g] = (lane == g)
    return jnp.asarray(m)


def _block_diag_mask(n, rows_per, cols_per):
    r = np.arange(n * rows_per)[:, None] // rows_per
    c = np.arange(n * cols_per)[None, :] // cols_per
    return jnp.asarray((r == c).astype(np.float32))


def _rope_tables(s, inv_freq, n_heads):
    ang = jnp.arange(s, dtype=F32)[:, None] * inv_freq[None, :]
    cos = jnp.cos(ang)
    sin = jnp.sin(ang)
    cos_t = jnp.tile(jnp.concatenate([cos, cos], axis=-1), (1, n_heads))
    sin_t = jnp.tile(jnp.concatenate([-sin, sin], axis=-1), (1, n_heads))
    return cos_t, sin_t


def kernel(x, attn_norm_g, w_in, moba_q_norm_g, moba_k_norm_g, ssm_conv_w, ssm_conv_b, ssm_dt_bias, ssm_a_log,
           ssm_d, ssm_norm_g, gla_gate_w2, gla_gate_b, gla_norm_g, ret_norm_g, w_out, ffn_norm_g, ffn_w_up,
           ffn_conv_w, ffn_conv_b, ffn_w_down):
    bsz, s, d = x.shape
    depth = w_in.shape[0]
    gw = GROUP_WIDTH

    moba_inv = ROPE_THETA ** (-jnp.arange(0, MOBA_HEAD_DIM, 2, dtype=F32) / MOBA_HEAD_DIM)
    moba_cos, moba_sin = _rope_tables(s, moba_inv, MOBA_HEADS)
    ret_inv = 1.0 / (ROPE_THETA ** jnp.linspace(0.0, 1.0, RET_DK // 2, dtype=F32))
    ret_cos, ret_sin = _rope_tables(s, ret_inv, RET_HEADS)
    head_mean = (_block_diag_mask(4, 64, 64) / 64.0).astype(BF16)
    hmask256 = _lane_group_mask(4, 64)
    qmask128 = _lane_group_mask(4, 32)
    ssm_gmask = _lane_group_mask(SSM_GROUPS, SSM_STATE)
    ssm_smask = _block_diag_mask(SSM_GROUPS, SSM_STATE, 2 * SSM_HEAD_DIM)
    lin_smask = _block_diag_mask(4, 32, 64)
    tri128 = jnp.asarray(np.tril(np.ones((SSM_CHUNK, SSM_CHUNK), np.float32))).astype(BF16)
    P = 2 * GLA_CHUNK
    same_chunk = (np.arange(P)[:, None] // GLA_CHUNK) == (np.arange(P)[None, :] // GLA_CHUNK)
    gla_tri = jnp.asarray((np.tril(np.ones((P, P))) * same_chunk).astype(np.float32)).astype(BF16)

    L = RET_CHUNK
    ret_lg = jnp.log(1.0 - 2.0 ** (-5.0 - jnp.arange(RET_HEADS, dtype=F32)))
    idx = jnp.arange(L, dtype=F32)
    diff = idx[:, None] - idx[None, :]
    ret_dmat = jnp.where(diff >= 0, jnp.exp(jnp.maximum(diff, 0.0)[None] * ret_lg[:, None, None]), 0.0)
    ret_dmat = ret_dmat.reshape(RET_HEADS * L, L)
    ret_kend = jnp.repeat(jnp.exp((L - 1 - idx)[:, None] * ret_lg[None, :]), RET_DK, axis=1)
    ret_qdec = jnp.repeat(jnp.exp((idx + 1.0)[:, None] * ret_lg[None, :]), RET_DK, axis=1)
    ret_cdec = jnp.repeat(jnp.exp(L * ret_lg), RET_DV)[None, :]

    o_mq, o_mk, o_mv = 0, 256, 512
    o_sz, o_sx, o_sd = 768, 1024, 1536
    o_gq, o_gk, o_gv, o_gr, o_gg = 1540, 1668, 1796, 2052, 2308
    o_rq = 2324

    w_in_t = jnp.transpose(w_in, (0, 2, 1))
    for l in range(depth):
        wl = w_in_t[l]
        w_moba_qk = wl[o_mq:o_mv].astype(BF16)
        w_moba_vt = wl[o_mv:o_sz].astype(BF16)
        w_ssm = jnp.concatenate(
            [wl[o_sz:o_sd], jnp.repeat(wl[o_sd:o_sd + SSM_HEADS], SSM_HEAD_DIM, axis=0)], axis=0).astype(BF16)
        w_gla = jnp.concatenate(
            [wl[o_gq:o_gg + GLA_GATE_RANK], jnp.zeros((LANES - GLA_GATE_RANK, d), F32)], axis=0).astype(BF16)
        w_ret = wl[o_rq:o_rq + 768].astype(BF16)
        ng = attn_norm_g[l][None, :]

        y_moba = _moba_call(x, ng, w_moba_qk, w_moba_vt, jnp.tile(moba_q_norm_g[l], MOBA_HEADS)[None, :],
                            jnp.tile(moba_k_norm_g[l], MOBA_HEADS)[None, :], moba_cos, moba_sin, head_mean,
                            hmask256)
        y_ssm = _ssd_call(x, ng, w_ssm, ssm_conv_w[l], ssm_conv_b[l][None, :],
                          jnp.repeat(ssm_dt_bias[l], SSM_HEAD_DIM)[None, :],
                          jnp.repeat(ssm_a_log[l], SSM_HEAD_DIM)[None, :],
                          jnp.repeat(ssm_d[l], SSM_HEAD_DIM)[None, :], ssm_norm_g[l][None, :], tri128, ssm_gmask,
                          hmask256, ssm_smask)
        w2 = jnp.concatenate([gla_gate_w2[l], jnp.zeros((LANES - GLA_GATE_RANK, GLA_HEADS * GLA_DK), F32)],
                             axis=0).astype(BF16)
        y_gla = _gla_call(x, ng, w_gla, w2, gla_gate_b[l][None, :], jnp.tile(gla_norm_g[l], GLA_HEADS)[None, :],
                          gla_tri, head_mean, qmask128, hmask256, lin_smask)
        y_ret = _ret_call(x, ng, w_ret, ret_cos, ret_sin, jnp.tile(ret_norm_g[l], RET_HEADS)[None, :], ret_dmat,
                          ret_kend, ret_qdec, ret_cdec, head_mean, qmask128, hmask256, lin_smask)
        x = _ffn_call(x, y_moba, y_ssm, y_gla, y_ret, w_out[l].astype(BF16), ffn_norm_g[l][None, :],
                      ffn_w_up[l].astype(BF16), ffn_conv_w[l], ffn_conv_b[l][None, :], ffn_w_down[l].astype(BF16))
    return x
```

```python
import functools

import numpy as np
import jax
import jax.numpy as jnp
from jax import lax
from jax.experimental import pallas as pl
from jax.experimental.pallas import tpu as pltpu

F32 = jnp.float32
BF16 = jnp.bfloat16
HIGHEST = lax.Precision.HIGHEST

D_MODEL = 1024
GROUP_WIDTH = 256
NORM_EPS = 1e-6
NEG_INF = -1e30

MOBA_HEADS = 4
MOBA_HEAD_DIM = 64
MOBA_BLOCK = 256
MOBA_TOPK = 3
MOBA_ONES_ROWS = 16
ROPE_THETA = 10000.0
LOG2_E = 1.4426950408889634

SSM_HEADS = 4
SSM_HEAD_DIM = 64
SSM_GROUPS = 2
SSM_STATE = 64
SSM_CONV = 4
SSM_CHUNK = 128
SSM_CONV_CH = 512

GLA_HEADS = 4
GLA_DV = 64
GLA_DK = 32
GLA_GATE_RANK = 16
GLA_GATE_TAU = 16.0
GLA_CHUNK = 64

RET_HEADS = 4
RET_DV = 64
RET_DK = 32
RET_CHUNK = 128

FFN_DIM = 2816
FFN_CONV = 3

LANES = 128
SUBLANES = 8
VMEM_LIMIT_BYTES = 48 * 1024 * 1024

MIX_ROWS = 2
REC_ROWS = 1
MIX_TILE = 512
FFN_TILE = 512
FFN_COLS = 256


def _rms_rows(x, g):
    ms = jnp.mean(x * x, axis=-1, keepdims=True)
    return x * lax.rsqrt(ms + NORM_EPS) * g


def _dot(a, b):
    return jnp.dot(a.astype(BF16), b.astype(BF16), preferred_element_type=F32)


def _dot_nt(a, b):
    return lax.dot_general(a.astype(BF16), b.astype(BF16), (((1,), (1,)), ((), ())),
                           preferred_element_type=F32)


def _silu(x):
    half = 0.5 * x
    return half + half * jnp.tanh(half)


def _softplus(x):
    return jnp.maximum(x, 0.0) + jnp.log(1.0 + jnp.exp(-jnp.abs(x)))


def _split_dot(a, b, passes, split_rhs=False):
    acc = None
    rest = b if split_rhs else a
    for j in range(passes):
        piece = rest.astype(BF16)
        part = jnp.dot(a if split_rhs else piece, piece if split_rhs else b, preferred_element_type=F32)
        acc = part if acc is None else acc + part
        if j + 1 < passes:
            rest = rest - piece.astype(F32)
    return acc


def _round_robin(generators):
    done = object()
    running = list(generators)
    while running:
        running = [gen for gen in running if next(gen, done) is not done]


def _rotate_half(t, half):
    slabs = []
    for c in range(t.shape[-1] // LANES):
        u = t[:, c * LANES:(c + 1) * LANES]
        lane = lax.broadcasted_iota(jnp.int32, u.shape, 1)
        first = (lane % (2 * half)) < half
        slabs.append(jnp.where(first, pltpu.roll(u, LANES - half, 1), pltpu.roll(u, half, 1)))
    return slabs[0] if len(slabs) == 1 else jnp.concatenate(slabs, axis=-1)


def _moba_prep(i, x_ref, g_ref, wqk_ref, wvt_ref, qg_ref, kg_ref, cos_ref, sin_ref, gm_ref, hmask_ref,
               k_s, vt_s, km_s, qm_s, bias_s, *, n_blk):
    blk = MOBA_BLOCK
    gw = GROUP_WIDTH
    dh = MOBA_HEAD_DIM

    h = _rms_rows(x_ref[...], g_ref[...]).astype(BF16)
    p = _dot_nt(h, wqk_ref[...])
    vt = _dot_nt(wvt_ref[...], h)
    gm = gm_ref[...]
    cos = cos_ref[...]
    sin = sin_ref[...]

    def prep(t, gain):
        ms = _split_dot(t * t, gm, 2)
        t = t * lax.rsqrt(ms + NORM_EPS) * gain
        return t * cos + _rotate_half(t, dh // 2) * sin

    yield
    q = prep(p[:, 0:gw], qg_ref[...])
    k = prep(p[:, gw:2 * gw], kg_ref[...])
    yield

    k_s[pl.ds(pl.multiple_of(i * blk, blk), blk), :] = k.astype(BF16)
    vtb = vt.astype(BF16)
    for hd in range(MOBA_HEADS):
        vt_s[i, hd, 0:dh, :] = vtb[hd * dh:(hd + 1) * dh, :]
        vt_s[i, hd, dh:dh + MOBA_ONES_ROWS, :] = jnp.ones((MOBA_ONES_ROWS, blk), BF16)
    km_s[i] = jnp.broadcast_to(jnp.mean(k, axis=0, keepdims=True), (SUBLANES, gw))

    row8 = lax.broadcasted_iota(jnp.int32, (SUBLANES, gw), 0)
    kmat = jnp.zeros((SUBLANES, gw), F32)
    for n in range(n_blk):
        kmat = jnp.where(row8 == n, km_s[n], kmat)
    hmask = hmask_ref[...]
    kmh = jnp.concatenate([kmat * hmask[hd:hd + 1, :] for hd in range(MOBA_HEADS)], axis=0)
    gate = lax.dot_general(kmh, q, (((1,), (1,)), ((), ())), precision=HIGHEST,
                           preferred_element_type=F32)
    scale = dh ** -0.5 * LOG2_E
    for hd in range(MOBA_HEADS):
        qm_s[hd] = (q * (scale * hmask[hd:hd + 1, :])).astype(BF16)
    yield

    rowq = lax.broadcasted_iota(jnp.int32, (SUBLANES, blk), 0)
    valid = rowq < i
    bias_tiles = [jnp.zeros((SUBLANES, blk), F32) for _ in range(n_blk)]
    for hd in range(MOBA_HEADS):
        gh = gate[hd * SUBLANES:(hd + 1) * SUBLANES, :]
        for n in range(n_blk):
            gn = gh[n:n + 1, :]
            beats = valid & ((gh > gn) | ((gh == gn) & (rowq < n)))
            cnt = jnp.sum(beats.astype(F32), axis=0, keepdims=True)
            bias_n = jnp.where(cnt < float(MOBA_TOPK), 0.0, NEG_INF)
            bias_tiles[n] = jnp.where(rowq == hd, bias_n, bias_tiles[n])
    for n in range(n_blk):
        bias_s[n] = bias_tiles[n]


def _moba_attend(ii, y_ref, k_s, vt_s, qm_s, bias_s, s_s):
    blk = MOBA_BLOCK
    dh = MOBA_HEAD_DIM
    rj = lax.broadcasted_iota(jnp.int32, (blk, blk), 0)
    cq = lax.broadcasted_iota(jnp.int32, (blk, blk), 1)
    causal = rj <= cq
    outs = []
    for hd in range(MOBA_HEADS):
        qh = qm_s[hd]
        m = None
        for n in range(ii + 1):
            s = _dot_nt(k_s[n * blk:(n + 1) * blk, :], qh)
            if n == ii:
                s = jnp.where(causal, s, NEG_INF)
            s_s[hd, n * blk:(n + 1) * blk, :] = s
            mx = jnp.max(s, axis=0, keepdims=True)
            if n < ii:
                mx = mx + bias_s[n, hd:hd + 1, :]
            m = mx if m is None else jnp.maximum(m, mx)
        acc = jnp.zeros((dh + MOBA_ONES_ROWS, blk), F32)
        for n in range(ii + 1):
            shift = m if n == ii else m - bias_s[n, hd:hd + 1, :]
            pr = jnp.exp2(s_s[hd, n * blk:(n + 1) * blk, :] - shift)
            acc = acc + jnp.dot(vt_s[n, hd], pr.astype(BF16), preferred_element_type=F32)
        outs.append(acc[0:dh, :] * (1.0 / acc[dh:dh + 1, :]))
        yield
    y_ref[...] = jnp.concatenate(outs, axis=0).T.astype(y_ref.dtype)


def _moba_kernel(*refs, n_blk, rows):
    i = pl.program_id(1)
    x_ref, shared, y_ref = refs[0], refs[1:10], refs[10]
    k_s, vt_s, km_s, qm_s, bias_s, s_s = refs[11:]

    @pl.when(i == 0)
    def _():
        km_s[...] = jnp.zeros_like(km_s)

    _round_robin([_moba_prep(i, x_ref.at[r], *shared, k_s.at[r], vt_s.at[r], km_s.at[r], qm_s.at[r],
                             bias_s.at[r], n_blk=n_blk) for r in range(rows)])

    def attend(ii):
        _round_robin([_moba_attend(ii, y_ref.at[r], k_s.at[r], vt_s.at[r], qm_s.at[r], bias_s.at[r], s_s.at[r])
                      for r in range(rows)])

    for ii in range(n_blk):
        pl.when(i == ii)(functools.partial(attend, ii))


def _moba_call(x, norm_g, wqk, wvt, qg, kg, cos_t, sin_t, gm, hmask):
    bsz, s, d = x.shape
    blk = MOBA_BLOCK
    n_blk = s // blk
    gw = GROUP_WIDTH
    rows = MIX_ROWS
    assert bsz % rows == 0
    const = lambda b, i: (0, 0)
    return pl.pallas_call(
        functools.partial(_moba_kernel, n_blk=n_blk, rows=rows),
        grid=(bsz // rows, n_blk),
        in_specs=[
            pl.BlockSpec((rows, blk, d), lambda b, i: (b, i, 0)),
            pl.BlockSpec((1, d), const),
            pl.BlockSpec((2 * gw, d), const),
            pl.BlockSpec((gw, d), const),
            pl.BlockSpec((1, gw), const),
            pl.BlockSpec((1, gw), const),
            pl.BlockSpec((blk, gw), lambda b, i: (i, 0)),
            pl.BlockSpec((blk, gw), lambda b, i: (i, 0)),
            pl.BlockSpec((gw, gw), const),
            pl.BlockSpec((SUBLANES, gw), const),
        ],
        out_specs=pl.BlockSpec((rows, blk, gw), lambda b, i: (b, i, 0)),
        out_shape=jax.ShapeDtypeStruct((bsz, s, gw), BF16),
        scratch_shapes=[
            pltpu.VMEM((rows, s, gw), BF16),
            pltpu.VMEM((rows, n_blk, MOBA_HEADS, MOBA_HEAD_DIM + MOBA_ONES_ROWS, blk), BF16),
            pltpu.VMEM((rows, n_blk, SUBLANES, gw), F32),
            pltpu.VMEM((rows, MOBA_HEADS, blk, gw), BF16),
            pltpu.VMEM((rows, n_blk, SUBLANES, blk), F32),
            pltpu.VMEM((rows, MOBA_HEADS, s, blk), F32),
        ],
        compiler_params=pltpu.CompilerParams(
            dimension_semantics=("parallel", "arbitrary"), vmem_limit_bytes=VMEM_LIMIT_BYTES),
    )(x, norm_g, wqk, wvt, qg, kg, cos_t, sin_t, gm, hmask)


def _whole(a):
    return pl.BlockSpec(a.shape, lambda b, t: (0,) * a.ndim)


def _mixers_kernel(streams, rows):
    def kernel(*refs):
        x_ref, g_ref = refs[0], refs[1]
        pos = 2
        shared = []
        for _, n_shared, _ in streams:
            shared.append(refs[pos:pos + n_shared])
            pos += n_shared
        outs = refs[pos:pos + len(streams)]
        pos += len(streams)
        scratch = []
        for _, _, n_scratch in streams:
            scratch.append(refs[pos:pos + n_scratch])
            pos += n_scratch

        @pl.when(pl.program_id(1) == 0)
        def _():
            for group in scratch:
                for s_ref in group:
                    s_ref[...] = jnp.zeros_like(s_ref)

        running = []
        for r in range(rows):
            h = _rms_rows(x_ref[r], g_ref[...]).astype(BF16)
            for (row_fn, _, _), sh, y_ref, sc in zip(streams, shared, outs, scratch):
                running.append(row_fn(h, *sh, y_ref.at[r], *(s_ref.at[r] for s_ref in sc)))
        _round_robin(running)

    return kernel


def _mixers_call(x, norm_g, specs, rows):
    bsz, s, d = x.shape
    tile = min(MIX_TILE, s)
    gw = GROUP_WIDTH
    assert bsz % rows == 0
    row_map = lambda b, t: (b, t, 0)
    in_specs = [pl.BlockSpec((rows, tile, d), row_map), pl.BlockSpec((1, d), lambda b, t: (0, 0))]
    arrays = [x, norm_g]
    scratch = []
    for _, arrs, blockspecs, scr in specs:
        arrays += list(arrs)
        in_specs += list(blockspecs)
        scratch += [pltpu.VMEM((rows,) + shape, dtype) for shape, dtype in scr]
    return pl.pallas_call(
        _mixers_kernel([(fn, len(arrs), len(scr)) for fn, arrs, _, scr in specs], rows),
        grid=(bsz // rows, s // tile),
        in_specs=in_specs,
        out_specs=[pl.BlockSpec((rows, tile, gw), row_map)] * len(specs),
        out_shape=[jax.ShapeDtypeStruct((bsz, s, gw), BF16)] * len(specs),
        scratch_shapes=scratch,
        compiler_params=pltpu.CompilerParams(
            dimension_semantics=("parallel", "arbitrary"), vmem_limit_bytes=VMEM_LIMIT_BYTES),
    )(*arrays)


def _ssd_row(h, w_ref, cw_ref, cb_ref, dtb_ref, alog_ref, dsk_ref, ng_ref, tri_ref,
             gmask_ref, hmask_ref, smask_ref, y_ref, xp_s, st_s, y_s, *, tile):
    gw = GROUP_WIDTH
    L = SSM_CHUNK
    gn = SSM_GROUPS * SSM_STATE

    p = _dot_nt(h, w_ref[...])
    z = p[:, 0:gw]
    xp_s[SUBLANES:SUBLANES + tile, :] = p[:, gw:gw + SSM_CONV_CH]
    cw = cw_ref[...]
    conv = cb_ref[...]
    for j in range(SSM_CONV):
        conv = conv + cw[j:j + 1, :] * xp_s[pl.ds(SUBLANES - (SSM_CONV - 1) + j, tile), :]
    xp_s[0:SUBLANES, :] = xp_s[tile:tile + SUBLANES, :]
    xbc = _silu(conv)
    xs = xbc[:, 0:gw]
    bm = xbc[:, gw:gw + gn]
    cm = xbc[:, gw + gn:gw + 2 * gn]
    dt = _softplus(p[:, gw + SSM_CONV_CH:2 * gw + SSM_CONV_CH] + dtb_ref[...])
    a = dt * (-jnp.exp(alog_ref[...]))
    xdt = xs * dt

    tri = tri_ref[...]
    gmask = gmask_ref[...]
    hmask = hmask_ref[...]
    smask = smask_ref[...]
    rl = lax.broadcasted_iota(jnp.int32, (L, L), 0)
    cs = lax.broadcasted_iota(jnp.int32, (L, L), 1)
    causal = rl >= cs
    rep = SSM_HEADS // SSM_GROUPS
    yield
    for c in range(tile // L):
        sl = slice(c * L, (c + 1) * L)
        a_cs = _split_dot(tri, a[sl], 3, split_rhs=True)
        a_cs_t = a_cs.T
        a_last = a_cs[L - 1:L, :]
        cmc = cm[sl]
        bmc = bm[sl]
        xdtc = xdt[sl]
        scores = _dot_nt(jnp.concatenate([cmc * gmask[g:g + 1, :] for g in range(SSM_GROUPS)], axis=0), bmc)
        weighted = []
        for hd in range(SSM_HEADS):
            g = hd // rep
            col = a_cs[:, hd * SSM_HEAD_DIM:hd * SSM_HEAD_DIM + 1]
            row = a_cs_t[hd * SSM_HEAD_DIM:hd * SSM_HEAD_DIM + 1, :]
            decay = jnp.where(causal, jnp.exp(col - row), 0.0)
            weighted.append(scores[g * L:(g + 1) * L, :] * decay)
        y_heads = _dot(jnp.concatenate(weighted, axis=0), xdtc)
        state = st_s[...]
        y = _dot(cmc, state) * jnp.exp(a_cs)
        for hd in range(SSM_HEADS):
            y = y + y_heads[hd * L:(hd + 1) * L, :] * hmask[hd:hd + 1, :]
        new = _dot(bmc.T, xdtc * jnp.exp(a_last - a_cs)) * smask
        st_s[...] = state * jnp.exp(a_last) + new
        y_s[sl, :] = y
        yield

    y = (y_s[...] + xs * dsk_ref[...]) * _silu(z)
    ng = ng_ref[...]
    half = gw // SSM_GROUPS
    for g in range(SSM_GROUPS):
        yg = y[:, g * half:(g + 1) * half]
        ms = jnp.mean(yg * yg, axis=-1, keepdims=True)
        y_ref[:, g * half:(g + 1) * half] = (yg * lax.rsqrt(ms + NORM_EPS) * ng[:, g * half:(g + 1) * half]
                                             ).astype(y_ref.dtype)


def _ssd_spec(tile, w, cw, cb, dtb, alog, dsk, ng, tri, gmask, hmask, smask):
    gw = GROUP_WIDTH
    gn = SSM_GROUPS * SSM_STATE
    arrays = (w, cw, cb, dtb, alog, dsk, ng, tri, gmask, hmask, smask)
    scratch = [((tile + SUBLANES, SSM_CONV_CH), F32),
               ((gn, gw), F32),
               ((tile, gw), F32)]
    return functools.partial(_ssd_row, tile=tile), arrays, [_whole(a) for a in arrays], scratch


def _gla_row(h, w_ref, w2_ref, b2_ref, ng_ref, tri_ref, gm_ref, qmask_ref,
             vmask_ref, smask_ref, y_ref, st_s, o_s, *, tile):
    gw = GROUP_WIDTH
    L = GLA_CHUNK
    P = 2 * L
    qk = GLA_HEADS * GLA_DK

    p = _dot_nt(h, w_ref[...])
    q = p[:, 0:qk] * (GLA_DK ** -0.5)
    k = p[:, qk:2 * qk]
    v = p[:, 2 * qk:2 * qk + gw]
    r = p[:, 2 * qk + gw:2 * qk + 2 * gw]
    g_pre = _dot(p[:, 2 * qk + 2 * gw:2 * qk + 2 * gw + LANES], w2_ref[...]) + b2_ref[...]
    lg = -_softplus(-g_pre) / GLA_GATE_TAU

    tri = tri_ref[...]
    qmask = qmask_ref[...]
    vmask = vmask_ref[...]
    smask = smask_ref[...]
    rl = lax.broadcasted_iota(jnp.int32, (P, P), 0)
    cs = lax.broadcasted_iota(jnp.int32, (P, P), 1)
    causal = (rl >= cs) & ((rl // L) == (cs // L))
    lane_p = lax.broadcasted_iota(jnp.int32, (qk, P), 1)
    row_p = lax.broadcasted_iota(jnp.int32, (P, qk), 0)
    yield
    for c in range(tile // P):
        sl = slice(c * P, (c + 1) * P)
        gc = lg[sl]
        bcs = _split_dot(tri, gc, 3, split_rhs=True)
        b_last = jnp.where(row_p < L, bcs[L - 1:L, :], bcs[P - 1:P, :])
        qd = q[sl] * jnp.exp(bcs)
        ki = k[sl] * jnp.exp(-bcs)
        ke_t = (k[sl] * jnp.exp(b_last - bcs)).T
        dec_t = jnp.exp(b_last).T
        vc = v[sl]
        o = jnp.zeros((P, gw), F32)
        for hd in range(GLA_HEADS):
            att = jnp.where(causal, _dot_nt(qd * qmask[hd:hd + 1, :], ki), 0.0)
            o = o + _dot(att, vc * vmask[hd:hd + 1, :])
        o_s[sl, :] = o
        for j in range(2):
            rows = slice(c * P + j * L, c * P + (j + 1) * L)
            state = st_s[...]
            o_s[rows, :] = o_s[rows, :] + _dot(qd[j * L:(j + 1) * L], state)
            in_chunk = (lane_p // L) == j
            new = _dot(jnp.where(in_chunk, ke_t, 0.0), vc) * smask
            st_s[...] = state * dec_t[:, j * L:j * L + 1] + new
        yield

    o = o_s[...]
    ms = _split_dot(o * o, gm_ref[...], 2)
    y_ref[...] = (o * lax.rsqrt(ms + NORM_EPS) * ng_ref[...] * _silu(r)).astype(y_ref.dtype)


def _gla_spec(tile, w, w2, b2, ng, tri, gm, qmask, vmask, smask):
    arrays = (w, w2, b2, ng, tri, gm, qmask, vmask, smask)
    scratch = [((GLA_HEADS * GLA_DK, GROUP_WIDTH), F32),
               ((tile, GROUP_WIDTH), F32)]
    return functools.partial(_gla_row, tile=tile), arrays, [_whole(a) for a in arrays], scratch


def _ret_row(h, w_ref, cos_ref, sin_ref, ng_ref, dmat_ref, kend_ref, qdec_ref, cdec_ref,
             gm_ref, qmask_ref, vmask_ref, smask_ref, y_ref, st_s, o_s, *, tile):
    gw = GROUP_WIDTH
    L = RET_CHUNK
    qk = RET_HEADS * RET_DK

    p = _dot_nt(h, w_ref[...])
    cos = cos_ref[...]
    sin = sin_ref[...]

    def rope(u):
        return u * cos + _rotate_half(u, RET_DK // 2) * sin

    q = rope(p[:, 0:qk])
    k = rope(p[:, qk:2 * qk]) * (RET_DK ** -0.5)
    v = p[:, 2 * qk:2 * qk + gw]
    gate = p[:, 2 * qk + gw:2 * qk + 2 * gw]

    dmat = dmat_ref[...]
    kend = kend_ref[...]
    qdec = qdec_ref[...]
    cdec = cdec_ref[...]
    qmask = qmask_ref[...]
    vmask = vmask_ref[...]
    smask = smask_ref[...]
    yield
    for c in range(tile // L):
        sl = slice(c * L, (c + 1) * L)
        qc = q[sl]
        kc = k[sl]
        vc = v[sl]
        q_heads = jnp.concatenate([qc * qmask[hd:hd + 1, :] for hd in range(RET_HEADS)], axis=0)
        o_heads = _dot(_dot_nt(q_heads, kc) * dmat, vc)
        state = st_s[...]
        o = _dot(qc * qdec, state)
        for hd in range(RET_HEADS):
            o = o + o_heads[hd * L:(hd + 1) * L, :] * vmask[hd:hd + 1, :]
        new = _dot((kc * kend).T, vc) * smask
        st_s[...] = state * cdec + new
        o_s[sl, :] = o
        yield

    o = o_s[...]
    ms = _split_dot(o * o, gm_ref[...], 2)
    y_ref[...] = (o * lax.rsqrt(ms + NORM_EPS) * ng_ref[...] * _silu(gate)).astype(y_ref.dtype)


def _ret_spec(tile, w, cos_t, sin_t, ng, dmat, kend, qdec, cdec, gm, qmask, vmask, smask):
    qk = RET_HEADS * RET_DK
    arrays = (w, cos_t, sin_t, ng, dmat, kend, qdec, cdec, gm, qmask, vmask, smask)
    by_tile = pl.BlockSpec((tile, qk), lambda b, t: (t, 0))
    specs = [_whole(w), by_tile, by_tile] + [_whole(a) for a in arrays[3:]]
    scratch = [((qk, GROUP_WIDTH), F32),
               ((tile, GROUP_WIDTH), F32)]
    return functools.partial(_ret_row, tile=tile), arrays, specs, scratch


def _ffn_kernel(x_ref, ya_ref, yb_ref, yc_ref, yd_ref, wo_ref, ng_ref, wg_ref, wv_ref, cwg_ref, cwv_ref,
                cbg_ref, cbv_ref, wd_ref, o_ref, h_s, cg_s, cv_s, *bufs, tile, nf):
    t = pl.program_id(1)
    gw = GROUP_WIDTH
    cols = FFN_COLS
    halo = SUBLANES
    a_bufs, u_bufs = bufs[0:2], bufs[2:]
    slots = (((u_bufs[0], u_bufs[1]), (u_bufs[2], u_bufs[3])), ((u_bufs[4], u_bufs[5]), (u_bufs[6], u_bufs[7])))
    groups = [(f, min(2, nf - f)) for f in range(0, nf, 2)]

    @pl.when(t == 0)
    def _():
        cg_s[...] = jnp.zeros_like(cg_s)
        cv_s[...] = jnp.zeros_like(cv_s)

    x1 = x_ref[...]
    for j, y_ref in enumerate((ya_ref, yb_ref, yc_ref, yd_ref)):
        x1 = x1 + jnp.dot(y_ref[...], wo_ref[j * gw:(j + 1) * gw, :], preferred_element_type=F32)
    o_ref[...] = x1
    h_s[...] = _rms_rows(x1, ng_ref[...]).astype(BF16)

    def up(f, ug_s, uv_s):
        h = h_s[...]
        for w_ref, u_s, c_s in ((wg_ref, ug_s, cg_s), (wv_ref, uv_s, cv_s)):
            u_s[halo:halo + tile, :] = jnp.dot(h, w_ref[f], preferred_element_type=F32)
            u_s[0:halo, :] = c_s[f]
            c_s[f] = u_s[tile:tile + halo, :]

    def activation(f, ug_s, uv_s):
        def conv(u_s, cw_ref, cb_ref):
            cw = cw_ref[f]
            y = cb_ref[f] + cw[FFN_CONV - 1:FFN_CONV, :] * u_s[halo:halo + tile, :]
            for j in range(FFN_CONV - 1):
                y = y + cw[j:j + 1, :] * u_s[pl.ds(halo - (FFN_CONV - 1) + j, tile), :]
            return y

        return (_silu(conv(ug_s, cwg_ref, cbg_ref)) * conv(uv_s, cwv_ref, cbv_ref)).astype(BF16)

    def act_group(gi):
        f, n = groups[gi]
        for k in range(n):
            a_bufs[gi % 2][:, k * cols:(k + 1) * cols] = activation(f + k, *slots[gi % 2][k])

    def down_group(gi):
        f, n = groups[gi]
        o_ref[...] += jnp.dot(a_bufs[gi % 2][:, 0:n * cols], wd_ref[f * cols:(f + n) * cols, :],
                              preferred_element_type=F32)

    for k in range(groups[0][1]):
        up(groups[0][0] + k, *slots[0][k])
    for gi in range(len(groups)):
        if gi + 1 < len(groups):
            f2, n2 = groups[gi + 1]
            for k in range(n2):
                up(f2 + k, *slots[(gi + 1) % 2][k])
        act_group(gi)
        if gi > 0:
            down_group(gi - 1)
    down_group(len(groups) - 1)


def _ffn_call(x, ya, yb, yc, yd, wo, ng, w_up, conv_w, conv_b, w_down):
    bsz, s, d = x.shape
    tile = min(FFN_TILE, s)
    cols = FFN_COLS
    nf = FFN_DIM // cols
    assert nf * cols == FFN_DIM
    gw = GROUP_WIDTH
    halo = SUBLANES
    w_up_t = w_up.reshape(d, 2 * nf, cols).transpose(1, 0, 2)
    cw_t = conv_w.reshape(FFN_CONV, 2 * nf, cols).transpose(1, 0, 2)
    cb_t = conv_b.reshape(2 * nf, 1, cols)
    row = lambda b, t: (b, t, 0)
    const2 = lambda b, t: (0, 0)
    lo = lambda b, t: (0, 0, 0)
    hi = lambda b, t: (1, 0, 0)
    once = pl.Buffered(1)
    return pl.pallas_call(
        functools.partial(_ffn_kernel, tile=tile, nf=nf),
        grid=(bsz, s // tile),
        in_specs=[
            pl.BlockSpec((None, tile, d), row),
            pl.BlockSpec((None, tile, gw), row),
            pl.BlockSpec((None, tile, gw), row),
            pl.BlockSpec((None, tile, gw), row),
            pl.BlockSpec((None, tile, gw), row),
            pl.BlockSpec((d, d), const2, pipeline_mode=once),
            pl.BlockSpec((1, d), const2),
            pl.BlockSpec((nf, d, cols), lo, pipeline_mode=once),
            pl.BlockSpec((nf, d, cols), hi, pipeline_mode=once),
            pl.BlockSpec((nf, FFN_CONV, cols), lo),
            pl.BlockSpec((nf, FFN_CONV, cols), hi),
            pl.BlockSpec((nf, 1, cols), lo),
            pl.BlockSpec((nf, 1, cols), hi),
            pl.BlockSpec((FFN_DIM, d), const2, pipeline_mode=once),
        ],
        out_specs=pl.BlockSpec((None, tile, d), row),
        out_shape=jax.ShapeDtypeStruct((bsz, s, d), F32),
        scratch_shapes=[
            pltpu.VMEM((tile, d), BF16),
            pltpu.VMEM((nf, halo, cols), F32),
            pltpu.VMEM((nf, halo, cols), F32),
        ] + [pltpu.VMEM((tile, 2 * cols), BF16)] * 2
        + [pltpu.VMEM((tile + halo, cols), F32)] * 8,
        compiler_params=pltpu.CompilerParams(
            dimension_semantics=("parallel", "arbitrary"), vmem_limit_bytes=VMEM_LIMIT_BYTES),
    )(x, ya, yb, yc, yd, wo, ng, w_up_t, w_up_t, cw_t, cw_t, cb_t, cb_t, w_down)


def _lane_group_mask(n_groups, width, rows=SUBLANES):
    lane = np.arange(n_groups * width) // width
    m = np.zeros((rows, n_groups * width), np.float32)
    for g in range(n_groups):
        m[g] = (lane == g)
    return jnp.asarray(m)


def _block_diag_mask(n, rows_per, cols_per):
    r = np.arange(n * rows_per)[:, None] // rows_per
    c = np.arange(n * cols_per)[None, :] // cols_per
    return jnp.asarray((r == c).astype(np.float32))


def _rope_tables(s, inv_freq, n_heads):
    ang = jnp.arange(s, dtype=F32)[:, None] * inv_freq[None, :]
    cos = jnp.cos(ang)
    sin = jnp.sin(ang)
    cos_t = jnp.tile(jnp.concatenate([cos, cos], axis=-1), (1, n_heads))
    sin_t = jnp.tile(jnp.concatenate([-sin, sin], axis=-1), (1, n_heads))
    return cos_t, sin_t


def kernel(x, attn_norm_g, w_in, moba_q_norm_g, moba_k_norm_g, ssm_conv_w, ssm_conv_b, ssm_dt_bias, ssm_a_log,
           ssm_d, ssm_norm_g, gla_gate_w2, gla_gate_b, gla_norm_g, ret_norm_g, w_out, ffn_norm_g, ffn_w_up,
           ffn_conv_w, ffn_conv_b, ffn_w_down):
    bsz, s, d = x.shape
    depth = w_in.shape[0]
    gw = GROUP_WIDTH

    moba_inv = ROPE_THETA ** (-jnp.arange(0, MOBA_HEAD_DIM, 2, dtype=F32) / MOBA_HEAD_DIM)
    moba_cos, moba_sin = _rope_tables(s, moba_inv, MOBA_HEADS)
    ret_inv = 1.0 / (ROPE_THETA ** jnp.linspace(0.0, 1.0, RET_DK // 2, dtype=F32))
    ret_cos, ret_sin = _rope_tables(s, ret_inv, RET_HEADS)
    head_mean = (_block_diag_mask(4, 64, 64) / 64.0).astype(BF16)
    hmask256 = _lane_group_mask(4, 64)
    qmask128 = _lane_group_mask(4, 32)
    ssm_gmask = _lane_group_mask(SSM_GROUPS, SSM_STATE)
    ssm_smask = _block_diag_mask(SSM_GROUPS, SSM_STATE, 2 * SSM_HEAD_DIM)
    lin_smask = _block_diag_mask(4, 32, 64)
    tri128 = jnp.asarray(np.tril(np.ones((SSM_CHUNK, SSM_CHUNK), np.float32))).astype(BF16)
    P = 2 * GLA_CHUNK
    same_chunk = (np.arange(P)[:, None] // GLA_CHUNK) == (np.arange(P)[None, :] // GLA_CHUNK)
    gla_tri = jnp.asarray((np.tril(np.ones((P, P))) * same_chunk).astype(np.float32)).astype(BF16)

    L = RET_CHUNK
    ret_lg = jnp.log(1.0 - 2.0 ** (-5.0 - jnp.arange(RET_HEADS, dtype=F32)))
    idx = jnp.arange(L, dtype=F32)
    diff = idx[:, None] - idx[None, :]
    ret_dmat = jnp.where(diff >= 0, jnp.exp(jnp.maximum(diff, 0.0)[None] * ret_lg[:, None, None]), 0.0)
    ret_dmat = ret_dmat.reshape(RET_HEADS * L, L)
    ret_kend = jnp.repeat(jnp.exp((L - 1 - idx)[:, None] * ret_lg[None, :]), RET_DK, axis=1)
    ret_qdec = jnp.repeat(jnp.exp((idx + 1.0)[:, None] * ret_lg[None, :]), RET_DK, axis=1)
    ret_cdec = jnp.repeat(jnp.exp(L * ret_lg), RET_DV)[None, :]

    o_mq, o_mk, o_mv = 0, 256, 512
    o_sz, o_sx, o_sd = 768, 1024, 1536
    o_gq, o_gk, o_gv, o_gr, o_gg = 1540, 1668, 1796, 2052, 2308
    o_rq = 2324

    w_in_t = jnp.transpose(w_in, (0, 2, 1))
    for l in range(depth):
        wl = w_in_t[l]
        w_moba_qk = wl[o_mq:o_mv].astype(BF16)
        w_moba_vt = wl[o_mv:o_sz].astype(BF16)
        w_ssm = jnp.concatenate(
            [wl[o_sz:o_sd], jnp.repeat(wl[o_sd:o_sd + SSM_HEADS], SSM_HEAD_DIM, axis=0)], axis=0).astype(BF16)
        w_gla = jnp.concatenate(
            [wl[o_gq:o_gg + GLA_GATE_RANK], jnp.zeros((LANES - GLA_GATE_RANK, d), F32)], axis=0).astype(BF16)
        w_ret = wl[o_rq:o_rq + 768].astype(BF16)
        ng = attn_norm_g[l][None, :]

        y_moba = _moba_call(x, ng, w_moba_qk, w_moba_vt, jnp.tile(moba_q_norm_g[l], MOBA_HEADS)[None, :],
                            jnp.tile(moba_k_norm_g[l], MOBA_HEADS)[None, :], moba_cos, moba_sin, head_mean,
                            hmask256)
        w2 = jnp.concatenate([gla_gate_w2[l], jnp.zeros((LANES - GLA_GATE_RANK, GLA_HEADS * GLA_DK), F32)],
                             axis=0).astype(BF16)
        tile = min(MIX_TILE, s)
        y_ssm, y_gla, y_ret = _mixers_call(x, ng, [
            _ssd_spec(tile, w_ssm, ssm_conv_w[l], ssm_conv_b[l][None, :],
                      jnp.repeat(ssm_dt_bias[l], SSM_HEAD_DIM)[None, :],
                      jnp.repeat(ssm_a_log[l], SSM_HEAD_DIM)[None, :],
                      jnp.repeat(ssm_d[l], SSM_HEAD_DIM)[None, :], ssm_norm_g[l][None, :], tri128, ssm_gmask,
                      hmask256, ssm_smask),
            _gla_spec(tile, w_gla, w2, gla_gate_b[l][None, :], jnp.tile(gla_norm_g[l], GLA_HEADS)[None, :],
                      gla_tri, head_mean, qmask128, hmask256, lin_smask),
            _ret_spec(tile, w_ret, ret_cos, ret_sin, jnp.tile(ret_norm_g[l], RET_HEADS)[None, :], ret_dmat,
                      ret_kend, ret_qdec, ret_cdec, head_mean, qmask128, hmask256, lin_smask),
        ], REC_ROWS)
        x = _ffn_call(x, y_moba, y_ssm, y_gla, y_ret, w_out[l].astype(BF16), ffn_norm_g[l][None, :],
                      ffn_w_up[l].astype(BF16), ffn_conv_w[l], ffn_conv_b[l][None, :], ffn_w_down[l].astype(BF16))
    return x
```

```python
import functools

import numpy as np
import jax
import jax.numpy as jnp
from jax import lax
from jax.experimental import pallas as pl
from jax.experimental.pallas import tpu as pltpu

F32 = jnp.float32
BF16 = jnp.bfloat16

D_MODEL = 1024
GROUP_WIDTH = 256
NORM_EPS = 1e-6
NEG_INF = -1e30

MOBA_HEADS = 4
MOBA_HEAD_DIM = 64
MOBA_BLOCK = 256
MOBA_TOPK = 3
MOBA_ONES_ROWS = 16
ROPE_THETA = 10000.0
LOG2_E = 1.4426950408889634

SSM_HEADS = 4
SSM_HEAD_DIM = 64
SSM_GROUPS = 2
SSM_STATE = 64
SSM_CONV = 4
SSM_CHUNK = 128
SSM_CONV_CH = 512

GLA_HEADS = 4
GLA_DV = 64
GLA_DK = 32
GLA_GATE_RANK = 16
GLA_GATE_TAU = 16.0
GLA_CHUNK = 64

RET_HEADS = 4
RET_DV = 64
RET_DK = 32
RET_CHUNK = 128

FFN_DIM = 2816
FFN_CONV = 3

LANES = 128
SUBLANES = 8
VMEM_LIMIT_BYTES = 48 * 1024 * 1024

MIX_ROWS = 4
MOBA_SCORE_BUFS = 2
REC_ROWS = 1
MIX_TILE = 512
FFN_TILE = 512
FFN_ROWS = 1
FFN_COLS = 256


def _rms_rows(x, g):
    ms = jnp.mean(x * x, axis=-1, keepdims=True)
    return x * lax.rsqrt(ms + NORM_EPS) * g


def _dot(a, b):
    return jnp.dot(a.astype(BF16), b.astype(BF16), preferred_element_type=F32)


def _dot_nt(a, b):
    return lax.dot_general(a.astype(BF16), b.astype(BF16), (((1,), (1,)), ((), ())),
                           preferred_element_type=F32)


def _silu(x):
    half = 0.5 * x
    return half + half * jnp.tanh(half)


def _softplus(x):
    return jnp.maximum(x, 0.0) + jnp.log(1.0 + jnp.exp(-jnp.abs(x)))


def _split_dot(a, b, passes, split_rhs=False):
    acc = None
    rest = b if split_rhs else a
    for j in range(passes):
        piece = rest.astype(BF16)
        part = jnp.dot(a if split_rhs else piece, piece if split_rhs else b, preferred_element_type=F32)
        acc = part if acc is None else acc + part
        if j + 1 < passes:
            rest = rest - piece.astype(F32)
    return acc


def _round_robin(generators):
    done = object()
    running = list(generators)
    while running:
        running = [gen for gen in running if next(gen, done) is not done]


def _rotate_half(t, half):
    slabs = []
    for c in range(t.shape[-1] // LANES):
        u = t[:, c * LANES:(c + 1) * LANES]
        lane = lax.broadcasted_iota(jnp.int32, u.shape, 1)
        first = (lane % (2 * half)) < half
        slabs.append(jnp.where(first, pltpu.roll(u, LANES - half, 1), pltpu.roll(u, half, 1)))
    return slabs[0] if len(slabs) == 1 else jnp.concatenate(slabs, axis=-1)


def _moba_prep(i, x_ref, g_ref, wqk_ref, wvt_ref, qg_ref, kg_ref, cos_ref, sin_ref, gm_ref, hmask_ref,
               k_s, vt_s, km_s, qm_s, bias_s, *, n_blk):
    blk = MOBA_BLOCK
    gw = GROUP_WIDTH
    dh = MOBA_HEAD_DIM

    h = _rms_rows(x_ref[...], g_ref[...]).astype(BF16)
    p = _dot_nt(h, wqk_ref[...])
    vt = _dot_nt(wvt_ref[...], h)
    gm = gm_ref[...]
    cos = cos_ref[...]
    sin = sin_ref[...]

    def prep(t, gain):
        ms = _split_dot(t * t, gm, 2)
        t = t * lax.rsqrt(ms + NORM_EPS) * gain
        return t * cos + _rotate_half(t, dh // 2) * sin

    yield
    q = prep(p[:, 0:gw], qg_ref[...])
    k = prep(p[:, gw:2 * gw], kg_ref[...])
    yield

    k_s[pl.ds(pl.multiple_of(i * blk, blk), blk), :] = k.astype(BF16)
    vtb = vt.astype(BF16)
    for hd in range(MOBA_HEADS):
        vt_s[i, hd, 0:dh, :] = vtb[hd * dh:(hd + 1) * dh, :]
        vt_s[i, hd, dh:dh + MOBA_ONES_ROWS, :] = jnp.ones((MOBA_ONES_ROWS, blk), BF16)
    km_s[i] = jnp.broadcast_to(jnp.mean(k, axis=0, keepdims=True), (SUBLANES, gw))

    row8 = lax.broadcasted_iota(jnp.int32, (SUBLANES, gw), 0)
    kmat = jnp.zeros((SUBLANES, gw), F32)
    for n in range(n_blk):
        kmat = jnp.where(row8 == n, km_s[n], kmat)
    hmask = hmask_ref[...]
    kmh = jnp.concatenate([kmat * hmask[hd:hd + 1, :] for hd in range(MOBA_HEADS)], axis=0)
    km_hi = kmh.astype(BF16)
    km_lo = (kmh - km_hi.astype(F32)).astype(BF16)
    q_hi = q.astype(BF16)
    q_lo = (q - q_hi.astype(F32)).astype(BF16)
    gate = _dot_nt(km_hi, q_hi) + _dot_nt(km_hi, q_lo) + _dot_nt(km_lo, q_hi)
    scale = dh ** -0.5 * LOG2_E
    for hd in range(MOBA_HEADS):
        qm_s[hd] = (q * (scale * hmask[hd:hd + 1, :])).astype(BF16)
    yield

    rowq = lax.broadcasted_iota(jnp.int32, (SUBLANES, blk), 0)
    valid = rowq < i
    bias_tiles = [jnp.zeros((SUBLANES, blk), F32) for _ in range(n_blk)]
    for hd in range(MOBA_HEADS):
        gh = gate[hd * SUBLANES:(hd + 1) * SUBLANES, :]
        for n in range(n_blk):
            gn = gh[n:n + 1, :]
            beats = valid & ((gh > gn) | ((gh == gn) & (rowq < n)))
            cnt = jnp.sum(beats.astype(F32), axis=0, keepdims=True)
            bias_n = jnp.where(cnt < float(MOBA_TOPK), 0.0, NEG_INF)
            bias_tiles[n] = jnp.where(rowq == hd, bias_n, bias_tiles[n])
    for n in range(n_blk):
        bias_s[n] = bias_tiles[n]


def _moba_attend(ii, y_ref, k_s, vt_s, qm_s, bias_s, s_s):
    blk = MOBA_BLOCK
    dh = MOBA_HEAD_DIM
    rj = lax.broadcasted_iota(jnp.int32, (blk, blk), 0)
    cq = lax.broadcasted_iota(jnp.int32, (blk, blk), 1)
    causal = rj <= cq
    outs = []
    for hd in range(MOBA_HEADS):
        qh = qm_s[hd]
        m = None
        for n in range(ii + 1):
            s = _dot_nt(k_s[n * blk:(n + 1) * blk, :], qh)
            if n == ii:
                s = jnp.where(causal, s, NEG_INF)
            s_s[hd % MOBA_SCORE_BUFS, n * blk:(n + 1) * blk, :] = s
            mx = jnp.max(s, axis=0, keepdims=True)
            if n < ii:
                mx = mx + bias_s[n, hd:hd + 1, :]
            m = mx if m is None else jnp.maximum(m, mx)
        acc = jnp.zeros((dh + MOBA_ONES_ROWS, blk), F32)
        for n in range(ii + 1):
            shift = m if n == ii else m - bias_s[n, hd:hd + 1, :]
            pr = jnp.exp2(s_s[hd % MOBA_SCORE_BUFS, n * blk:(n + 1) * blk, :] - shift)
            acc = acc + jnp.dot(vt_s[n, hd], pr.astype(BF16), preferred_element_type=F32)
        outs.append(acc[0:dh, :] * (1.0 / acc[dh:dh + 1, :]))
        yield
    y_ref[...] = jnp.concatenate(outs, axis=0).T.astype(y_ref.dtype)


def _moba_kernel(*refs, n_blk, rows):
    i = pl.program_id(1)
    x_ref, shared, y_ref = refs[0], refs[1:10], refs[10]
    k_s, vt_s, km_s, qm_s, bias_s, s_s = refs[11:]

    @pl.when(i == 0)
    def _():
        km_s[...] = jnp.zeros_like(km_s)

    _round_robin([_moba_prep(i, x_ref.at[r], *shared, k_s.at[r], vt_s.at[r], km_s.at[r], qm_s.at[r],
                             bias_s.at[r], n_blk=n_blk) for r in range(rows)])

    def attend(ii):
        _round_robin([_moba_attend(ii, y_ref.at[r], k_s.at[r], vt_s.at[r], qm_s.at[r], bias_s.at[r], s_s.at[r])
                      for r in range(rows)])

    for ii in range(n_blk):
        pl.when(i == ii)(functools.partial(attend, ii))


def _moba_call(x, norm_g, wqk, wvt, qg, kg, cos_t, sin_t, gm, hmask):
    bsz, s, d = x.shape
    blk = MOBA_BLOCK
    n_blk = s // blk
    gw = GROUP_WIDTH
    rows = MIX_ROWS
    assert bsz % rows == 0
    const = lambda b, i: (0, 0)
    return pl.pallas_call(
        functools.partial(_moba_kernel, n_blk=n_blk, rows=rows),
        grid=(bsz // rows, n_blk),
        in_specs=[
            pl.BlockSpec((rows, blk, d), lambda b, i: (b, i, 0)),
            pl.BlockSpec((1, d), const),
            pl.BlockSpec((2 * gw, d), const),
            pl.BlockSpec((gw, d), const),
            pl.BlockSpec((1, gw), const),
            pl.BlockSpec((1, gw), const),
            pl.BlockSpec((blk, gw), lambda b, i: (i, 0)),
            pl.BlockSpec((blk, gw), lambda b, i: (i, 0)),
            pl.BlockSpec((gw, gw), const),
            pl.BlockSpec((SUBLANES, gw), const),
        ],
        out_specs=pl.BlockSpec((rows, blk, gw), lambda b, i: (b, i, 0)),
        out_shape=jax.ShapeDtypeStruct((bsz, s, gw), BF16),
        scratch_shapes=[
            pltpu.VMEM((rows, s, gw), BF16),
            pltpu.VMEM((rows, n_blk, MOBA_HEADS, MOBA_HEAD_DIM + MOBA_ONES_ROWS, blk), BF16),
            pltpu.VMEM((rows, n_blk, SUBLANES, gw), F32),
            pltpu.VMEM((rows, MOBA_HEADS, blk, gw), BF16),
            pltpu.VMEM((rows, n_blk, SUBLANES, blk), F32),
            pltpu.VMEM((rows, MOBA_SCORE_BUFS, s, blk), F32),
        ],
        compiler_params=pltpu.CompilerParams(
            dimension_semantics=("parallel", "arbitrary"), vmem_limit_bytes=VMEM_LIMIT_BYTES),
    )(x, norm_g, wqk, wvt, qg, kg, cos_t, sin_t, gm, hmask)


def _whole(a):
    return pl.BlockSpec(a.shape, lambda b, t: (0,) * a.ndim)


def _mixers_kernel(streams, rows):
    def kernel(*refs):
        x_ref, g_ref = refs[0], refs[1]
        pos = 2
        shared = []
        for _, n_shared, _ in streams:
            shared.append(refs[pos:pos + n_shared])
            pos += n_shared
        outs = refs[pos:pos + len(streams)]
        pos += len(streams)
        scratch = []
        for _, _, n_scratch in streams:
            scratch.append(refs[pos:pos + n_scratch])
            pos += n_scratch

        @pl.when(pl.program_id(1) == 0)
        def _():
            for group in scratch:
                for s_ref in group:
                    s_ref[...] = jnp.zeros_like(s_ref)

        running = []
        for r in range(rows):
            h = _rms_rows(x_ref[r], g_ref[...]).astype(BF16)
            for (row_fn, _, _), sh, y_ref, sc in zip(streams, shared, outs, scratch):
                running.append(row_fn(h, *sh, y_ref.at[r], *(s_ref.at[r] for s_ref in sc)))
        _round_robin(running)

    return kernel


def _mixers_call(x, norm_g, specs, rows):
    bsz, s, d = x.shape
    tile = min(MIX_TILE, s)
    gw = GROUP_WIDTH
    assert bsz % rows == 0
    row_map = lambda b, t: (b, t, 0)
    in_specs = [pl.BlockSpec((rows, tile, d), row_map), pl.BlockSpec((1, d), lambda b, t: (0, 0))]
    arrays = [x, norm_g]
    scratch = []
    for _, arrs, blockspecs, scr in specs:
        arrays += list(arrs)
        in_specs += list(blockspecs)
        scratch += [pltpu.VMEM((rows,) + shape, dtype) for shape, dtype in scr]
    return pl.pallas_call(
        _mixers_kernel([(fn, len(arrs), len(scr)) for fn, arrs, _, scr in specs], rows),
        grid=(bsz // rows, s // tile),
        in_specs=in_specs,
        out_specs=[pl.BlockSpec((rows, tile, gw), row_map)] * len(specs),
        out_shape=[jax.ShapeDtypeStruct((bsz, s, gw), BF16)] * len(specs),
        scratch_shapes=scratch,
        compiler_params=pltpu.CompilerParams(
            dimension_semantics=("parallel", "arbitrary"), vmem_limit_bytes=VMEM_LIMIT_BYTES),
    )(*arrays)


def _ssd_row(h, w_ref, cw_ref, cb_ref, dtb_ref, alog_ref, dsk_ref, ng_ref, tri_ref,
             gmask_ref, hmask_ref, smask_ref, y_ref, xp_s, st_s, y_s, *, tile):
    gw = GROUP_WIDTH
    L = SSM_CHUNK
    gn = SSM_GROUPS * SSM_STATE

    p = _dot_nt(h, w_ref[...])
    z = p[:, 0:gw]
    xp_s[SUBLANES:SUBLANES + tile, :] = p[:, gw:gw + SSM_CONV_CH]
    cw = cw_ref[...]
    conv = cb_ref[...]
    for j in range(SSM_CONV):
        conv = conv + cw[j:j + 1, :] * xp_s[pl.ds(SUBLANES - (SSM_CONV - 1) + j, tile), :]
    xp_s[0:SUBLANES, :] = xp_s[tile:tile + SUBLANES, :]
    xbc = _silu(conv)
    xs = xbc[:, 0:gw]
    bm = xbc[:, gw:gw + gn]
    cm = xbc[:, gw + gn:gw + 2 * gn]
    dt = _softplus(p[:, gw + SSM_CONV_CH:2 * gw + SSM_CONV_CH] + dtb_ref[...])
    a = dt * (-jnp.exp(alog_ref[...]))
    xdt = xs * dt

    tri = tri_ref[...]
    gmask = gmask_ref[...]
    hmask = hmask_ref[...]
    smask = smask_ref[...]
    rl = lax.broadcasted_iota(jnp.int32, (L, L), 0)
    cs = lax.broadcasted_iota(jnp.int32, (L, L), 1)
    causal = rl >= cs
    rep = SSM_HEADS // SSM_GROUPS
    yield
    for c in range(tile // L):
        sl = slice(c * L, (c + 1) * L)
        a_cs = _split_dot(tri, a[sl], 3, split_rhs=True)
        a_cs_t = a_cs.T
        a_last = a_cs[L - 1:L, :]
        cmc = cm[sl]
        bmc = bm[sl]
        xdtc = xdt[sl]
        scores = _dot_nt(cmc, jnp.concatenate([bmc * gmask[g:g + 1, :] for g in range(SSM_GROUPS)], axis=0))
        weighted = []
        for hd in range(SSM_HEADS):
            g = hd // rep
            col = a_cs[:, hd * SSM_HEAD_DIM:hd * SSM_HEAD_DIM + 1]
            row = a_cs_t[hd * SSM_HEAD_DIM:hd * SSM_HEAD_DIM + 1, :]
            decay = jnp.where(causal, jnp.exp(col - row), 0.0)
            weighted.append(scores[:, g * L:(g + 1) * L] * decay)
        x_heads = jnp.concatenate([xdtc * hmask[hd:hd + 1, :] for hd in range(SSM_HEADS)], axis=0)
        state = st_s[...]
        y = _dot(jnp.concatenate(weighted, axis=1), x_heads) + _dot(cmc, state) * jnp.exp(a_cs)
        new = _dot(bmc.T, xdtc * jnp.exp(a_last - a_cs)) * smask
        st_s[...] = state * jnp.exp(a_last) + new
        y_s[sl, :] = y
        yield

    y = (y_s[...] + xs * dsk_ref[...]) * _silu(z)
    ng = ng_ref[...]
    half = gw // SSM_GROUPS
    for g in range(SSM_GROUPS):
        yg = y[:, g * half:(g + 1) * half]
        ms = jnp.mean(yg * yg, axis=-1, keepdims=True)
        y_ref[:, g * half:(g + 1) * half] = (yg * lax.rsqrt(ms + NORM_EPS) * ng[:, g * half:(g + 1) * half]
                                             ).astype(y_ref.dtype)


def _ssd_spec(tile, w, cw, cb, dtb, alog, dsk, ng, tri, gmask, hmask, smask):
    gw = GROUP_WIDTH
    gn = SSM_GROUPS * SSM_STATE
    arrays = (w, cw, cb, dtb, alog, dsk, ng, tri, gmask, hmask, smask)
    scratch = [((tile + SUBLANES, SSM_CONV_CH), F32),
               ((gn, gw), F32),
               ((tile, gw), F32)]
    return functools.partial(_ssd_row, tile=tile), arrays, [_whole(a) for a in arrays], scratch


def _gla_row(h, w_ref, w2_ref, b2_ref, ng_ref, tri_ref, gm_ref, qmask_ref,
             vmask_ref, smask_ref, y_ref, st_s, o_s, *, tile):
    gw = GROUP_WIDTH
    L = GLA_CHUNK
    P = 2 * L
    qk = GLA_HEADS * GLA_DK

    p = _dot_nt(h, w_ref[...])
    q = p[:, 0:qk] * (GLA_DK ** -0.5)
    k = p[:, qk:2 * qk]
    v = p[:, 2 * qk:2 * qk + gw]
    r = p[:, 2 * qk + gw:2 * qk + 2 * gw]
    g_pre = _dot(p[:, 2 * qk + 2 * gw:2 * qk + 2 * gw + LANES], w2_ref[...]) + b2_ref[...]
    lg = -_softplus(-g_pre) / GLA_GATE_TAU

    tri = tri_ref[...]
    qmask = qmask_ref[...]
    vmask = vmask_ref[...]
    smask = smask_ref[...]
    rl = lax.broadcasted_iota(jnp.int32, (P, GLA_HEADS * P), 0)
    cs = lax.broadcasted_iota(jnp.int32, (P, GLA_HEADS * P), 1) % P
    causal = (rl >= cs) & ((rl // L) == (cs // L))
    lane_p = lax.broadcasted_iota(jnp.int32, (qk, P), 1)
    row_p = lax.broadcasted_iota(jnp.int32, (P, qk), 0)
    yield
    for c in range(tile // P):
        sl = slice(c * P, (c + 1) * P)
        gc = lg[sl]
        bcs = _split_dot(tri, gc, 3, split_rhs=True)
        b_last = jnp.where(row_p < L, bcs[L - 1:L, :], bcs[P - 1:P, :])
        qd = q[sl] * jnp.exp(bcs)
        ki = k[sl] * jnp.exp(-bcs)
        ke_t = (k[sl] * jnp.exp(b_last - bcs)).T
        dec_t = jnp.exp(b_last).T
        vc = v[sl]
        k_heads = jnp.concatenate([jnp.where(qmask[hd:hd + 1, :] > 0.5, ki, 0.0) for hd in range(GLA_HEADS)],
                                  axis=0)
        v_heads = jnp.concatenate([vc * vmask[hd:hd + 1, :] for hd in range(GLA_HEADS)], axis=0)
        o_s[sl, :] = _dot(jnp.where(causal, _dot_nt(qd, k_heads), 0.0), v_heads)
        for j in range(2):
            rows = slice(c * P + j * L, c * P + (j + 1) * L)
            state = st_s[...]
            o_s[rows, :] = o_s[rows, :] + _dot(qd[j * L:(j + 1) * L], state)
            in_chunk = (lane_p // L) == j
            new = _dot(jnp.where(in_chunk, ke_t, 0.0), vc) * smask
            st_s[...] = state * dec_t[:, j * L:j * L + 1] + new
        yield

    o = o_s[...]
    ms = _split_dot(o * o, gm_ref[...], 2)
    y_ref[...] = (o * lax.rsqrt(ms + NORM_EPS) * ng_ref[...] * _silu(r)).astype(y_ref.dtype)


def _gla_spec(tile, w, w2, b2, ng, tri, gm, qmask, vmask, smask):
    arrays = (w, w2, b2, ng, tri, gm, qmask, vmask, smask)
    scratch = [((GLA_HEADS * GLA_DK, GROUP_WIDTH), F32),
               ((tile, GROUP_WIDTH), F32)]
    return functools.partial(_gla_row, tile=tile), arrays, [_whole(a) for a in arrays], scratch


def _ret_row(h, w_ref, cos_ref, sin_ref, ng_ref, dmat_ref, kend_ref, qdec_ref, cdec_ref,
             gm_ref, qmask_ref, vmask_ref, smask_ref, y_ref, st_s, o_s, *, tile):
    gw = GROUP_WIDTH
    L = RET_CHUNK
    qk = RET_HEADS * RET_DK

    p = _dot_nt(h, w_ref[...])
    cos = cos_ref[...]
    sin = sin_ref[...]

    def rope(u):
        return u * cos + _rotate_half(u, RET_DK // 2) * sin

    q = rope(p[:, 0:qk])
    k = rope(p[:, qk:2 * qk]) * (RET_DK ** -0.5)
    v = p[:, 2 * qk:2 * qk + gw]
    gate = p[:, 2 * qk + gw:2 * qk + 2 * gw]

    dmat = dmat_ref[...]
    kend = kend_ref[...]
    qdec = qdec_ref[...]
    cdec = cdec_ref[...]
    qmask = qmask_ref[...]
    vmask = vmask_ref[...]
    smask = smask_ref[...]
    yield
    for c in range(tile // L):
        sl = slice(c * L, (c + 1) * L)
        qc = q[sl]
        kc = k[sl]
        vc = v[sl]
        k_heads = jnp.concatenate([kc * qmask[hd:hd + 1, :] for hd in range(RET_HEADS)], axis=0)
        v_heads = jnp.concatenate([vc * vmask[hd:hd + 1, :] for hd in range(RET_HEADS)], axis=0)
        state = st_s[...]
        o = _dot(_dot_nt(qc, k_heads) * dmat, v_heads) + _dot(qc * qdec, state)
        new = _dot((kc * kend).T, vc) * smask
        st_s[...] = state * cdec + new
        o_s[sl, :] = o
        yield

    o = o_s[...]
    ms = _split_dot(o * o, gm_ref[...], 2)
    y_ref[...] = (o * lax.rsqrt(ms + NORM_EPS) * ng_ref[...] * _silu(gate)).astype(y_ref.dtype)


def _ret_spec(tile, w, cos_t, sin_t, ng, dmat, kend, qdec, cdec, gm, qmask, vmask, smask):
    qk = RET_HEADS * RET_DK
    arrays = (w, cos_t, sin_t, ng, dmat, kend, qdec, cdec, gm, qmask, vmask, smask)
    by_tile = pl.BlockSpec((tile, qk), lambda b, t: (t, 0))
    specs = [_whole(w), by_tile, by_tile] + [_whole(a) for a in arrays[3:]]
    scratch = [((qk, GROUP_WIDTH), F32),
               ((tile, GROUP_WIDTH), F32)]
    return functools.partial(_ret_row, tile=tile), arrays, specs, scratch


def _ffn_kernel(*refs, tile, nf, rows):
    acts, shared, o_ref, scratch = refs[0:5], refs[5:14], refs[14], refs[15:]

    @pl.when(pl.program_id(1) == 0)
    def _():
        for c_s in scratch[1:3]:
            c_s[...] = jnp.zeros_like(c_s)

    _round_robin([_ffn_row(*(a.at[r] for a in acts), *shared, o_ref.at[r], *(s_ref.at[r] for s_ref in scratch),
                           tile=tile, nf=nf) for r in range(rows)])


def _ffn_row(x_ref, ya_ref, yb_ref, yc_ref, yd_ref, wo_ref, ng_ref, wg_ref, wv_ref, cwg_ref, cwv_ref,
             cbg_ref, cbv_ref, wd_ref, o_ref, h_s, cg_s, cv_s, *bufs, tile, nf):
    gw = GROUP_WIDTH
    cols = FFN_COLS
    halo = SUBLANES
    a_bufs, u_bufs = bufs[0:2], bufs[2:]
    slots = (((u_bufs[0], u_bufs[1]), (u_bufs[2], u_bufs[3])), ((u_bufs[4], u_bufs[5]), (u_bufs[6], u_bufs[7])))
    groups = [(f, min(2, nf - f)) for f in range(0, nf, 2)]

    x1 = x_ref[...]
    for j, y_ref in enumerate((ya_ref, yb_ref, yc_ref, yd_ref)):
        x1 = x1 + jnp.dot(y_ref[...], wo_ref[j * gw:(j + 1) * gw, :], preferred_element_type=F32)
    o_ref[...] = x1
    h_s[...] = _rms_rows(x1, ng_ref[...]).astype(BF16)
    yield

    def up(f, ug_s, uv_s):
        h = h_s[...]
        for w_ref, u_s, c_s in ((wg_ref, ug_s, cg_s), (wv_ref, uv_s, cv_s)):
            u_s[halo:halo + tile, :] = jnp.dot(h, w_ref[f], preferred_element_type=F32)
            u_s[0:halo, :] = c_s[f]
            c_s[f] = u_s[tile:tile + halo, :]

    def activation(f, ug_s, uv_s):
        def conv(u_s, cw_ref, cb_ref):
            cw = cw_ref[f]
            y = cb_ref[f] + cw[FFN_CONV - 1:FFN_CONV, :] * u_s[halo:halo + tile, :]
            for j in range(FFN_CONV - 1):
                y = y + cw[j:j + 1, :] * u_s[pl.ds(halo - (FFN_CONV - 1) + j, tile), :]
            return y

        return (_silu(conv(ug_s, cwg_ref, cbg_ref)) * conv(uv_s, cwv_ref, cbv_ref)).astype(BF16)

    def act_group(gi):
        f, n = groups[gi]
        for k in range(n):
            a_bufs[gi % 2][:, k * cols:(k + 1) * cols] = activation(f + k, *slots[gi % 2][k])

    def down_group(gi):
        f, n = groups[gi]
        o_ref[...] += jnp.dot(a_bufs[gi % 2][:, 0:n * cols], wd_ref[f * cols:(f + n) * cols, :],
                              preferred_element_type=F32)

    for k in range(groups[0][1]):
        up(groups[0][0] + k, *slots[0][k])
    for gi in range(len(groups)):
        if gi + 1 < len(groups):
            f2, n2 = groups[gi + 1]
            for k in range(n2):
                up(f2 + k, *slots[(gi + 1) % 2][k])
        act_group(gi)
        if gi > 0:
            down_group(gi - 1)
        yield
    down_group(len(groups) - 1)


def _ffn_call(x, ya, yb, yc, yd, wo, ng, w_up, conv_w, conv_b, w_down):
    bsz, s, d = x.shape
    tile = min(FFN_TILE, s)
    cols = FFN_COLS
    nf = FFN_DIM // cols
    assert nf * cols == FFN_DIM
    gw = GROUP_WIDTH
    halo = SUBLANES
    w_up_t = w_up.reshape(d, 2 * nf, cols).transpose(1, 0, 2)
    cw_t = conv_w.reshape(FFN_CONV, 2 * nf, cols).transpose(1, 0, 2)
    cb_t = conv_b.reshape(2 * nf, 1, cols)
    row = lambda b, t: (b, t, 0)
    const2 = lambda b, t: (0, 0)
    lo = lambda b, t: (0, 0, 0)
    hi = lambda b, t: (1, 0, 0)
    once = pl.Buffered(1)
    rows = FFN_ROWS
    assert bsz % rows == 0
    return pl.pallas_call(
        functools.partial(_ffn_kernel, tile=tile, nf=nf, rows=rows),
        grid=(bsz // rows, s // tile),
        in_specs=[
            pl.BlockSpec((rows, tile, d), row),
            pl.BlockSpec((rows, tile, gw), row),
            pl.BlockSpec((rows, tile, gw), row),
            pl.BlockSpec((rows, tile, gw), row),
            pl.BlockSpec((rows, tile, gw), row),
            pl.BlockSpec((d, d), const2, pipeline_mode=once),
            pl.BlockSpec((1, d), const2),
            pl.BlockSpec((nf, d, cols), lo, pipeline_mode=once),
            pl.BlockSpec((nf, d, cols), hi, pipeline_mode=once),
            pl.BlockSpec((nf, FFN_CONV, cols), lo),
            pl.BlockSpec((nf, FFN_CONV, cols), hi),
            pl.BlockSpec((nf, 1, cols), lo),
            pl.BlockSpec((nf, 1, cols), hi),
            pl.BlockSpec((FFN_DIM, d), const2, pipeline_mode=once),
        ],
        out_specs=pl.BlockSpec((rows, tile, d), row),
        out_shape=jax.ShapeDtypeStruct((bsz, s, d), F32),
        scratch_shapes=[
            pltpu.VMEM((rows, tile, d), BF16),
            pltpu.VMEM((rows, nf, halo, cols), F32),
            pltpu.VMEM((rows, nf, halo, cols), F32),
        ] + [pltpu.VMEM((rows, tile, 2 * cols), BF16)] * 2
        + [pltpu.VMEM((rows, tile + halo, cols), F32)] * 8,
        compiler_params=pltpu.CompilerParams(
            dimension_semantics=("parallel", "arbitrary"), vmem_limit_bytes=VMEM_LIMIT_BYTES),
    )(x, ya, yb, yc, yd, wo, ng, w_up_t, w_up_t, cw_t, cw_t, cb_t, cb_t, w_down)


def _lane_group_mask(n_groups, width, rows=SUBLANES):
    lane = np.arange(n_groups * width) // width
    m = np.zeros((rows, n_groups * width), np.float32)
    for g in range(n_groups):
        m[g] = (lane == g)
    return jnp.asarray(m)


def _block_diag_mask(n, rows_per, cols_per):
    r = np.arange(n * rows_per)[:, None] // rows_per
    c = np.arange(n * cols_per)[None, :] // cols_per
    return jnp.asarray((r == c).astype(np.float32))


def _rope_tables(s, inv_freq, n_heads):
    ang = jnp.arange(s, dtype=F32)[:, None] * inv_freq[None, :]
    cos = jnp.cos(ang)
    sin = jnp.sin(ang)
    cos_t = jnp.tile(jnp.concatenate([cos, cos], axis=-1), (1, n_heads))
    sin_t = jnp.tile(jnp.concatenate([-sin, sin], axis=-1), (1, n_heads))
    return cos_t, sin_t


def kernel(x, attn_norm_g, w_in, moba_q_norm_g, moba_k_norm_g, ssm_conv_w, ssm_conv_b, ssm_dt_bias, ssm_a_log,
           ssm_d, ssm_norm_g, gla_gate_w2, gla_gate_b, gla_norm_g, ret_norm_g, w_out, ffn_norm_g, ffn_w_up,
           ffn_conv_w, ffn_conv_b, ffn_w_down):
    bsz, s, d = x.shape
    depth = w_in.shape[0]
    gw = GROUP_WIDTH

    moba_inv = ROPE_THETA ** (-jnp.arange(0, MOBA_HEAD_DIM, 2, dtype=F32) / MOBA_HEAD_DIM)
    moba_cos, moba_sin = _rope_tables(s, moba_inv, MOBA_HEADS)
    ret_inv = 1.0 / (ROPE_THETA ** jnp.linspace(0.0, 1.0, RET_DK // 2, dtype=F32))
    ret_cos, ret_sin = _rope_tables(s, ret_inv, RET_HEADS)
    head_mean = (_block_diag_mask(4, 64, 64) / 64.0).astype(BF16)
    hmask256 = _lane_group_mask(4, 64)
    qmask128 = _lane_group_mask(4, 32)
    ssm_gmask = _lane_group_mask(SSM_GROUPS, SSM_STATE)
    ssm_smask = _block_diag_mask(SSM_GROUPS, SSM_STATE, 2 * SSM_HEAD_DIM)
    lin_smask = _block_diag_mask(4, 32, 64)
    tri128 = jnp.asarray(np.tril(np.ones((SSM_CHUNK, SSM_CHUNK), np.float32))).astype(BF16)
    P = 2 * GLA_CHUNK
    same_chunk = (np.arange(P)[:, None] // GLA_CHUNK) == (np.arange(P)[None, :] // GLA_CHUNK)
    gla_tri = jnp.asarray((np.tril(np.ones((P, P))) * same_chunk).astype(np.float32)).astype(BF16)

    L = RET_CHUNK
    ret_lg = jnp.log(1.0 - 2.0 ** (-5.0 - jnp.arange(RET_HEADS, dtype=F32)))
    idx = jnp.arange(L, dtype=F32)
    diff = idx[:, None] - idx[None, :]
    ret_dmat = jnp.where(diff >= 0, jnp.exp(jnp.maximum(diff, 0.0)[None] * ret_lg[:, None, None]), 0.0)
    ret_dmat = ret_dmat.transpose(1, 0, 2).reshape(L, RET_HEADS * L)
    ret_kend = jnp.repeat(jnp.exp((L - 1 - idx)[:, None] * ret_lg[None, :]), RET_DK, axis=1)
    ret_qdec = jnp.repeat(jnp.exp((idx + 1.0)[:, None] * ret_lg[None, :]), RET_DK, axis=1)
    ret_cdec = jnp.repeat(jnp.exp(L * ret_lg), RET_DV)[None, :]

    o_mq, o_mk, o_mv = 0, 256, 512
    o_sz, o_sx, o_sd = 768, 1024, 1536
    o_gq, o_gk, o_gv, o_gr, o_gg = 1540, 1668, 1796, 2052, 2308
    o_rq = 2324

    w_in_t = jnp.transpose(w_in, (0, 2, 1))
    for l in range(depth):
        wl = w_in_t[l]
        w_moba_qk = wl[o_mq:o_mv].astype(BF16)
        w_moba_vt = wl[o_mv:o_sz].astype(BF16)
        w_ssm = jnp.concatenate(
            [wl[o_sz:o_sd], jnp.repeat(wl[o_sd:o_sd + SSM_HEADS], SSM_HEAD_DIM, axis=0)], axis=0).astype(BF16)
        w_gla = jnp.concatenate(
            [wl[o_gq:o_gg + GLA_GATE_RANK], jnp.zeros((LANES - GLA_GATE_RANK, d), F32)], axis=0).astype(BF16)
        w_ret = wl[o_rq:o_rq + 768].astype(BF16)
        ng = attn_norm_g[l][None, :]

        y_moba = _moba_call(x, ng, w_moba_qk, w_moba_vt, jnp.tile(moba_q_norm_g[l], MOBA_HEADS)[None, :],
                            jnp.tile(moba_k_norm_g[l], MOBA_HEADS)[None, :], moba_cos, moba_sin, head_mean,
                            hmask256)
        w2 = jnp.concatenate([gla_gate_w2[l], jnp.zeros((LANES - GLA_GATE_RANK, GLA_HEADS * GLA_DK), F32)],
                             axis=0).astype(BF16)
        tile = min(MIX_TILE, s)
        y_ssm, y_gla, y_ret = _mixers_call(x, ng, [
            _ssd_spec(tile, w_ssm, ssm_conv_w[l], ssm_conv_b[l][None, :],
                      jnp.repeat(ssm_dt_bias[l], SSM_HEAD_DIM)[None, :],
                      jnp.repeat(ssm_a_log[l], SSM_HEAD_DIM)[None, :],
                      jnp.repeat(ssm_d[l], SSM_HEAD_DIM)[None, :], ssm_norm_g[l][None, :], tri128, ssm_gmask,
                      hmask256, ssm_smask),
            _gla_spec(tile, w_gla, w2, gla_gate_b[l][None, :], jnp.tile(gla_norm_g[l], GLA_HEADS)[None, :],
                      gla_tri, head_mean, qmask128, hmask256, lin_smask),
            _ret_spec(tile, w_ret, ret_cos, ret_sin, jnp.tile(ret_norm_g[l], RET_HEADS)[None, :], ret_dmat,
                      ret_kend, ret_qdec, ret_cdec, head_mean, qmask128, hmask256, lin_smask),
        ], REC_ROWS)
        x = _ffn_call(x, y_moba, y_ssm, y_gla, y_ret, w_out[l].astype(BF16), ffn_norm_g[l][None, :],
                      ffn_w_up[l].astype(BF16), ffn_conv_w[l], ffn_conv_b[l][None, :], ffn_w_down[l].astype(BF16))
    return x
```

```python
import functools

import numpy as np
import jax
import jax.numpy as jnp
from jax import lax
from jax.experimental import pallas as pl
from jax.experimental.pallas import tpu as pltpu

F32 = jnp.float32
BF16 = jnp.bfloat16

D_MODEL = 1024
GROUP_WIDTH = 256
NORM_EPS = 1e-6
NEG_INF = -1e30

MOBA_HEADS = 4
MOBA_HEAD_DIM = 64
MOBA_BLOCK = 256
MOBA_TOPK = 3
MOBA_ONES_ROWS = 16
ROPE_THETA = 10000.0
LOG2_E = 1.4426950408889634

SSM_HEADS = 4
SSM_HEAD_DIM = 64
SSM_GROUPS = 2
SSM_STATE = 64
SSM_CONV = 4
SSM_CHUNK = 128
SSM_CONV_CH = 512

GLA_HEADS = 4
GLA_DV = 64
GLA_DK = 32
GLA_GATE_RANK = 16
GLA_GATE_TAU = 16.0
GLA_CHUNK = 64

RET_HEADS = 4
RET_DV = 64
RET_DK = 32
RET_CHUNK = 128

FFN_DIM = 2816
FFN_CONV = 3

LANES = 128
SUBLANES = 8
VMEM_LIMIT_BYTES = 48 * 1024 * 1024

MIX_ROWS = 2
REC_ROWS = 1
MIX_TILE = 512
FFN_TILE = 512
FFN_COLS = 256


def _rms_rows(x, g):
    ms = jnp.mean(x * x, axis=-1, keepdims=True)
    return x * lax.rsqrt(ms + NORM_EPS) * g


def _dot(a, b):
    return jnp.dot(a.astype(BF16), b.astype(BF16), preferred_element_type=F32)


def _dot_nt(a, b):
    return lax.dot_general(a.astype(BF16), b.astype(BF16), (((1,), (1,)), ((), ())),
                           preferred_element_type=F32)


def _silu(x):
    half = 0.5 * x
    return half + half * jnp.tanh(half)


def _softplus(x):
    return jnp.maximum(x, 0.0) + jnp.log(1.0 + jnp.exp(-jnp.abs(x)))


def _split_dot(a, b, passes, split_rhs=False):
    acc = None
    rest = b if split_rhs else a
    for j in range(passes):
        piece = rest.astype(BF16)
        part = jnp.dot(a if split_rhs else piece, piece if split_rhs else b, preferred_element_type=F32)
        acc = part if acc is None else acc + part
        if j + 1 < passes:
            rest = rest - piece.astype(F32)
    return acc


def _round_robin(generators):
    done = object()
    running = list(generators)
    while running:
        running = [gen for gen in running if next(gen, done) is not done]


def _rotate_half(t, half):
    slabs = []
    for c in range(t.shape[-1] // LANES):
        u = t[:, c * LANES:(c + 1) * LANES]
        lane = lax.broadcasted_iota(jnp.int32, u.shape, 1)
        first = (lane % (2 * half)) < half
        slabs.append(jnp.where(first, pltpu.roll(u, LANES - half, 1), pltpu.roll(u, half, 1)))
    return slabs[0] if len(slabs) == 1 else jnp.concatenate(slabs, axis=-1)


def _moba_prep(i, x_ref, g_ref, wqk_ref, wvt_ref, qg_ref, kg_ref, cos_ref, sin_ref, gm_ref, hmask_ref,
               k_s, vt_s, km_s, qm_s, bias_s, *, n_blk):
    blk = MOBA_BLOCK
    gw = GROUP_WIDTH
    dh = MOBA_HEAD_DIM

    h = _rms_rows(x_ref[...], g_ref[...]).astype(BF16)
    p = _dot_nt(h, wqk_ref[...])
    vt = _dot_nt(wvt_ref[...], h)
    gm = gm_ref[...]
    cos = cos_ref[...]
    sin = sin_ref[...]

    def prep(t, gain):
        ms = _split_dot(t * t, gm, 2)
        t = t * lax.rsqrt(ms + NORM_EPS) * gain
        return t * cos + _rotate_half(t, dh // 2) * sin

    yield
    q = prep(p[:, 0:gw], qg_ref[...])
    k = prep(p[:, gw:2 * gw], kg_ref[...])
    yield

    k_s[pl.ds(pl.multiple_of(i * blk, blk), blk), :] = k.astype(BF16)
    vtb = vt.astype(BF16)
    for hd in range(MOBA_HEADS):
        vt_s[i, hd, 0:dh, :] = vtb[hd * dh:(hd + 1) * dh, :]
        vt_s[i, hd, dh:dh + MOBA_ONES_ROWS, :] = jnp.ones((MOBA_ONES_ROWS, blk), BF16)
    km_s[i] = jnp.broadcast_to(jnp.mean(k, axis=0, keepdims=True), (SUBLANES, gw))

    row8 = lax.broadcasted_iota(jnp.int32, (SUBLANES, gw), 0)
    kmat = jnp.zeros((SUBLANES, gw), F32)
    for n in range(n_blk):
        kmat = jnp.where(row8 == n, km_s[n], kmat)
    hmask = hmask_ref[...]
    kmh = jnp.concatenate([kmat * hmask[hd:hd + 1, :] for hd in range(MOBA_HEADS)], axis=0)
    km_hi = kmh.astype(BF16)
    km_lo = (kmh - km_hi.astype(F32)).astype(BF16)
    q_hi = q.astype(BF16)
    q_lo = (q - q_hi.astype(F32)).astype(BF16)
    gate = _dot_nt(km_hi, q_hi) + _dot_nt(km_hi, q_lo) + _dot_nt(km_lo, q_hi)
    scale = dh ** -0.5 * LOG2_E
    for hd in range(MOBA_HEADS):
        qm_s[hd] = (q * (scale * hmask[hd:hd + 1, :])).astype(BF16)
    yield

    rowq = lax.broadcasted_iota(jnp.int32, (SUBLANES, blk), 0)
    valid = rowq < i
    bias_tiles = [jnp.zeros((SUBLANES, blk), F32) for _ in range(n_blk)]
    for hd in range(MOBA_HEADS):
        gh = gate[hd * SUBLANES:(hd + 1) * SUBLANES, :]
        for n in range(n_blk):
            gn = gh[n:n + 1, :]
            beats = valid & ((gh > gn) | ((gh == gn) & (rowq < n)))
            cnt = jnp.sum(beats.astype(F32), axis=0, keepdims=True)
            bias_n = jnp.where(cnt < float(MOBA_TOPK), 0.0, NEG_INF)
            bias_tiles[n] = jnp.where(rowq == hd, bias_n, bias_tiles[n])
    for n in range(n_blk):
        bias_s[n] = bias_tiles[n]


def _moba_attend(ii, y_ref, k_s, vt_s, qm_s, bias_s, s_s):
    blk = MOBA_BLOCK
    dh = MOBA_HEAD_DIM
    rj = lax.broadcasted_iota(jnp.int32, (blk, blk), 0)
    cq = lax.broadcasted_iota(jnp.int32, (blk, blk), 1)
    causal = rj <= cq
    outs = []
    for hd in range(MOBA_HEADS):
        qh = qm_s[hd]
        m = None
        for n in range(ii + 1):
            s = _dot_nt(k_s[n * blk:(n + 1) * blk, :], qh)
            if n == ii:
                s = jnp.where(causal, s, NEG_INF)
            s_s[hd, n * blk:(n + 1) * blk, :] = s
            mx = jnp.max(s, axis=0, keepdims=True)
            if n < ii:
                mx = mx + bias_s[n, hd:hd + 1, :]
            m = mx if m is None else jnp.maximum(m, mx)
        acc = jnp.zeros((dh + MOBA_ONES_ROWS, blk), F32)
        for n in range(ii + 1):
            shift = m if n == ii else m - bias_s[n, hd:hd + 1, :]
            pr = jnp.exp2(s_s[hd, n * blk:(n + 1) * blk, :] - shift)
            acc = acc + jnp.dot(vt_s[n, hd], pr.astype(BF16), preferred_element_type=F32)
        outs.append(acc[0:dh, :] * (1.0 / acc[dh:dh + 1, :]))
        yield
    y_ref[...] = jnp.concatenate(outs, axis=0).T.astype(y_ref.dtype)


def _moba_kernel(*refs, n_blk, rows):
    i = pl.program_id(1)
    x_ref, shared, y_ref = refs[0], refs[1:10], refs[10]
    k_s, vt_s, km_s, qm_s, bias_s, s_s = refs[11:]

    @pl.when(i == 0)
    def _():
        km_s[...] = jnp.zeros_like(km_s)

    _round_robin([_moba_prep(i, x_ref.at[r], *shared, k_s.at[r], vt_s.at[r], km_s.at[r], qm_s.at[r],
                             bias_s.at[r], n_blk=n_blk) for r in range(rows)])

    def attend(ii):
        _round_robin([_moba_attend(ii, y_ref.at[r], k_s.at[r], vt_s.at[r], qm_s.at[r], bias_s.at[r], s_s.at[r])
                      for r in range(rows)])

    for ii in range(n_blk):
        pl.when(i == ii)(functools.partial(attend, ii))


def _moba_call(x, norm_g, wqk, wvt, qg, kg, cos_t, sin_t, gm, hmask):
    bsz, s, d = x.shape
    blk = MOBA_BLOCK
    n_blk = s // blk
    gw = GROUP_WIDTH
    rows = MIX_ROWS
    assert bsz % rows == 0
    const = lambda b, i: (0, 0)
    return pl.pallas_call(
        functools.partial(_moba_kernel, n_blk=n_blk, rows=rows),
        grid=(bsz // rows, n_blk),
        in_specs=[
            pl.BlockSpec((rows, blk, d), lambda b, i: (b, i, 0)),
            pl.BlockSpec((1, d), const),
            pl.BlockSpec((2 * gw, d), const),
            pl.BlockSpec((gw, d), const),
            pl.BlockSpec((1, gw), const),
            pl.BlockSpec((1, gw), const),
            pl.BlockSpec((blk, gw), lambda b, i: (i, 0)),
            pl.BlockSpec((blk, gw), lambda b, i: (i, 0)),
            pl.BlockSpec((gw, gw), const),
            pl.BlockSpec((SUBLANES, gw), const),
        ],
        out_specs=pl.BlockSpec((rows, blk, gw), lambda b, i: (b, i, 0)),
        out_shape=jax.ShapeDtypeStruct((bsz, s, gw), BF16),
        scratch_shapes=[
            pltpu.VMEM((rows, s, gw), BF16),
            pltpu.VMEM((rows, n_blk, MOBA_HEADS, MOBA_HEAD_DIM + MOBA_ONES_ROWS, blk), BF16),
            pltpu.VMEM((rows, n_blk, SUBLANES, gw), F32),
            pltpu.VMEM((rows, MOBA_HEADS, blk, gw), BF16),
            pltpu.VMEM((rows, n_blk, SUBLANES, blk), F32),
            pltpu.VMEM((rows, MOBA_HEADS, s, blk), F32),
        ],
        compiler_params=pltpu.CompilerParams(
            dimension_semantics=("parallel", "arbitrary"), vmem_limit_bytes=VMEM_LIMIT_BYTES),
    )(x, norm_g, wqk, wvt, qg, kg, cos_t, sin_t, gm, hmask)


def _whole(a):
    return pl.BlockSpec(a.shape, lambda b, t: (0,) * a.ndim)


def _mixers_kernel(streams, rows):
    def kernel(*refs):
        x_ref, g_ref = refs[0], refs[1]
        pos = 2
        shared = []
        for _, n_shared, _ in streams:
            shared.append(refs[pos:pos + n_shared])
            pos += n_shared
        outs = refs[pos:pos + len(streams)]
        pos += len(streams)
        scratch = []
        for _, _, n_scratch in streams:
            scratch.append(refs[pos:pos + n_scratch])
            pos += n_scratch

        @pl.when(pl.program_id(1) == 0)
        def _():
            for group in scratch:
                for s_ref in group:
                    s_ref[...] = jnp.zeros_like(s_ref)

        running = []
        for r in range(rows):
            h = _rms_rows(x_ref[r], g_ref[...]).astype(BF16)
            for (row_fn, _, _), sh, y_ref, sc in zip(streams, shared, outs, scratch):
                running.append(row_fn(h, *sh, y_ref.at[r], *(s_ref.at[r] for s_ref in sc)))
        _round_robin(running)

    return kernel


def _mixers_call(x, norm_g, specs, rows):
    bsz, s, d = x.shape
    tile = min(MIX_TILE, s)
    gw = GROUP_WIDTH
    assert bsz % rows == 0
    row_map = lambda b, t: (b, t, 0)
    in_specs = [pl.BlockSpec((rows, tile, d), row_map), pl.BlockSpec((1, d), lambda b, t: (0, 0))]
    arrays = [x, norm_g]
    scratch = []
    for _, arrs, blockspecs, scr in specs:
        arrays += list(arrs)
        in_specs += list(blockspecs)
        scratch += [pltpu.VMEM((rows,) + shape, dtype) for shape, dtype in scr]
    return pl.pallas_call(
        _mixers_kernel([(fn, len(arrs), len(scr)) for fn, arrs, _, scr in specs], rows),
        grid=(bsz // rows, s // tile),
        in_specs=in_specs,
        out_specs=[pl.BlockSpec((rows, tile, gw), row_map)] * len(specs),
        out_shape=[jax.ShapeDtypeStruct((bsz, s, gw), BF16)] * len(specs),
        scratch_shapes=scratch,
        compiler_params=pltpu.CompilerParams(
            dimension_semantics=("parallel", "arbitrary"), vmem_limit_bytes=VMEM_LIMIT_BYTES),
    )(*arrays)


def _ssd_row(h, w_ref, cw_ref, cb_ref, dtb_ref, alog_ref, dsk_ref, ng_ref, tri_ref,
             gmask_ref, hmask_ref, smask_ref, y_ref, xp_s, st_s, y_s, *, tile):
    gw = GROUP_WIDTH
    L = SSM_CHUNK
    gn = SSM_GROUPS * SSM_STATE

    p = _dot_nt(h, w_ref[...])
    z = p[:, 0:gw]
    xp_s[SUBLANES:SUBLANES + tile, :] = p[:, gw:gw + SSM_CONV_CH]
    cw = cw_ref[...]
    conv = cb_ref[...]
    for j in range(SSM_CONV):
        conv = conv + cw[j:j + 1, :] * xp_s[pl.ds(SUBLANES - (SSM_CONV - 1) + j, tile), :]
    xp_s[0:SUBLANES, :] = xp_s[tile:tile + SUBLANES, :]
    xbc = _silu(conv)
    xs = xbc[:, 0:gw]
    bm = xbc[:, gw:gw + gn]
    cm = xbc[:, gw + gn:gw + 2 * gn]
    dt = _softplus(p[:, gw + SSM_CONV_CH:2 * gw + SSM_CONV_CH] + dtb_ref[...])
    a = dt * (-jnp.exp(alog_ref[...]))
    xdt = xs * dt

    tri = tri_ref[...]
    gmask = gmask_ref[...]
    hmask = hmask_ref[...]
    smask = smask_ref[...]
    rl = lax.broadcasted_iota(jnp.int32, (L, L), 0)
    cs = lax.broadcasted_iota(jnp.int32, (L, L), 1)
    causal = rl >= cs
    rep = SSM_HEADS // SSM_GROUPS
    yield
    for c in range(tile // L):
        sl = slice(c * L, (c + 1) * L)
        a_cs = _split_dot(tri, a[sl], 3, split_rhs=True)
        a_cs_t = a_cs.T
        a_last = a_cs[L - 1:L, :]
        cmc = cm[sl]
        bmc = bm[sl]
        xdtc = xdt[sl]
        scores = _dot_nt(jnp.concatenate([cmc * gmask[g:g + 1, :] for g in range(SSM_GROUPS)], axis=0), bmc)
        weighted = []
        for hd in range(SSM_HEADS):
            g = hd // rep
            col = a_cs[:, hd * SSM_HEAD_DIM:hd * SSM_HEAD_DIM + 1]
            row = a_cs_t[hd * SSM_HEAD_DIM:hd * SSM_HEAD_DIM + 1, :]
            decay = jnp.where(causal, jnp.exp(col - row), 0.0)
            weighted.append(scores[g * L:(g + 1) * L, :] * decay)
        y_heads = _dot(jnp.concatenate(weighted, axis=0), xdtc)
        state = st_s[...]
        y = _dot(cmc, state) * jnp.exp(a_cs)
        for hd in range(SSM_HEADS):
            y = y + y_heads[hd * L:(hd + 1) * L, :] * hmask[hd:hd + 1, :]
        new = _dot(bmc.T, xdtc * jnp.exp(a_last - a_cs)) * smask
        st_s[...] = state * jnp.exp(a_last) + new
        y_s[sl, :] = y
        yield

    y = (y_s[...] + xs * dsk_ref[...]) * _silu(z)
    ng = ng_ref[...]
    half = gw // SSM_GROUPS
    for g in range(SSM_GROUPS):
        yg = y[:, g * half:(g + 1) * half]
        ms = jnp.mean(yg * yg, axis=-1, keepdims=True)
        y_ref[:, g * half:(g + 1) * half] = (yg * lax.rsqrt(ms + NORM_EPS) * ng[:, g * half:(g + 1) * half]
                                             ).astype(y_ref.dtype)


def _ssd_spec(tile, w, cw, cb, dtb, alog, dsk, ng, tri, gmask, hmask, smask):
    gw = GROUP_WIDTH
    gn = SSM_GROUPS * SSM_STATE
    arrays = (w, cw, cb, dtb, alog, dsk, ng, tri, gmask, hmask, smask)
    scratch = [((tile + SUBLANES, SSM_CONV_CH), F32),
               ((gn, gw), F32),
               ((tile, gw), F32)]
    return functools.partial(_ssd_row, tile=tile), arrays, [_whole(a) for a in arrays], scratch


def _gla_row(h, w_ref, w2_ref, b2_ref, ng_ref, tri_ref, gm_ref, qmask_ref,
             vmask_ref, smask_ref, y_ref, st_s, o_s, *, tile):
    gw = GROUP_WIDTH
    L = GLA_CHUNK
    P = 2 * L
    qk = GLA_HEADS * GLA_DK

    p = _dot_nt(h, w_ref[...])
    q = p[:, 0:qk] * (GLA_DK ** -0.5)
    k = p[:, qk:2 * qk]
    v = p[:, 2 * qk:2 * qk + gw]
    r = p[:, 2 * qk + gw:2 * qk + 2 * gw]
    g_pre = _dot(p[:, 2 * qk + 2 * gw:2 * qk + 2 * gw + LANES], w2_ref[...]) + b2_ref[...]
    lg = -_softplus(-g_pre) / GLA_GATE_TAU

    tri = tri_ref[...]
    qmask = qmask_ref[...]
    vmask = vmask_ref[...]
    smask = smask_ref[...]
    rl = lax.broadcasted_iota(jnp.int32, (P, P), 0)
    cs = lax.broadcasted_iota(jnp.int32, (P, P), 1)
    causal = (rl >= cs) & ((rl // L) == (cs // L))
    lane_p = lax.broadcasted_iota(jnp.int32, (qk, P), 1)
    row_p = lax.broadcasted_iota(jnp.int32, (P, qk), 0)
    yield
    for c in range(tile // P):
        sl = slice(c * P, (c + 1) * P)
        gc = lg[sl]
        bcs = _split_dot(tri, gc, 3, split_rhs=True)
        b_last = jnp.where(row_p < L, bcs[L - 1:L, :], bcs[P - 1:P, :])
        qd = q[sl] * jnp.exp(bcs)
        ki = k[sl] * jnp.exp(-bcs)
        ke_t = (k[sl] * jnp.exp(b_last - bcs)).T
        dec_t = jnp.exp(b_last).T
        vc = v[sl]
        o = jnp.zeros((P, gw), F32)
        for hd in range(GLA_HEADS):
            att = jnp.where(causal, _dot_nt(qd * qmask[hd:hd + 1, :], ki), 0.0)
            o = o + _dot(att, vc * vmask[hd:hd + 1, :])
        o_s[sl, :] = o
        for j in range(2):
            rows = slice(c * P + j * L, c * P + (j + 1) * L)
            state = st_s[...]
            o_s[rows, :] = o_s[rows, :] + _dot(qd[j * L:(j + 1) * L], state)
            in_chunk = (lane_p // L) == j
            new = _dot(jnp.where(in_chunk, ke_t, 0.0), vc) * smask
            st_s[...] = state * dec_t[:, j * L:j * L + 1] + new
        yield

    o = o_s[...]
    ms = _split_dot(o * o, gm_ref[...], 2)
    y_ref[...] = (o * lax.rsqrt(ms + NORM_EPS) * ng_ref[...] * _silu(r)).astype(y_ref.dtype)


def _gla_spec(tile, w, w2, b2, ng, tri, gm, qmask, vmask, smask):
    arrays = (w, w2, b2, ng, tri, gm, qmask, vmask, smask)
    scratch = [((GLA_HEADS * GLA_DK, GROUP_WIDTH), F32),
               ((tile, GROUP_WIDTH), F32)]
    return functools.partial(_gla_row, tile=tile), arrays, [_whole(a) for a in arrays], scratch


def _ret_row(h, w_ref, cos_ref, sin_ref, ng_ref, dmat_ref, kend_ref, qdec_ref, cdec_ref,
             gm_ref, qmask_ref, vmask_ref, smask_ref, y_ref, st_s, o_s, *, tile):
    gw = GROUP_WIDTH
    L = RET_CHUNK
    qk = RET_HEADS * RET_DK

    p = _dot_nt(h, w_ref[...])
    cos = cos_ref[...]
    sin = sin_ref[...]

    def rope(u):
        return u * cos + _rotate_half(u, RET_DK // 2) * sin

    q = rope(p[:, 0:qk])
    k = rope(p[:, qk:2 * qk]) * (RET_DK ** -0.5)
    v = p[:, 2 * qk:2 * qk + gw]
    gate = p[:, 2 * qk + gw:2 * qk + 2 * gw]

    dmat = dmat_ref[...]
    kend = kend_ref[...]
    qdec = qdec_ref[...]
    cdec = cdec_ref[...]
    qmask = qmask_ref[...]
    vmask = vmask_ref[...]
    smask = smask_ref[...]
    yield
    for c in range(tile // L):
        sl = slice(c * L, (c + 1) * L)
        qc = q[sl]
        kc = k[sl]
        vc = v[sl]
        q_heads = jnp.concatenate([qc * qmask[hd:hd + 1, :] for hd in range(RET_HEADS)], axis=0)
        o_heads = _dot(_dot_nt(q_heads, kc) * dmat, vc)
        state = st_s[...]
        o = _dot(qc * qdec, state)
        for hd in range(RET_HEADS):
            o = o + o_heads[hd * L:(hd + 1) * L, :] * vmask[hd:hd + 1, :]
        new = _dot((kc * kend).T, vc) * smask
        st_s[...] = state * cdec + new
        o_s[sl, :] = o
        yield

    o = o_s[...]
    ms = _split_dot(o * o, gm_ref[...], 2)
    y_ref[...] = (o * lax.rsqrt(ms + NORM_EPS) * ng_ref[...] * _silu(gate)).astype(y_ref.dtype)


def _ret_spec(tile, w, cos_t, sin_t, ng, dmat, kend, qdec, cdec, gm, qmask, vmask, smask):
    qk = RET_HEADS * RET_DK
    arrays = (w, cos_t, sin_t, ng, dmat, kend, qdec, cdec, gm, qmask, vmask, smask)
    by_tile = pl.BlockSpec((tile, qk), lambda b, t: (t, 0))
    specs = [_whole(w), by_tile, by_tile] + [_whole(a) for a in arrays[3:]]
    scratch = [((qk, GROUP_WIDTH), F32),
               ((tile, GROUP_WIDTH), F32)]
    return functools.partial(_ret_row, tile=tile), arrays, specs, scratch


def _ffn_kernel(x_ref, ya_ref, yb_ref, yc_ref, yd_ref, wo_ref, ng_ref, wg_ref, wv_ref, cwg_ref, cwv_ref,
                cbg_ref, cbv_ref, wd_ref, o_ref, h_s, cg_s, cv_s, *bufs, tile, nf):
    t = pl.program_id(1)
    gw = GROUP_WIDTH
    cols = FFN_COLS
    halo = SUBLANES
    a_bufs, u_bufs = bufs[0:2], bufs[2:]
    slots = (((u_bufs[0], u_bufs[1]), (u_bufs[2], u_bufs[3])), ((u_bufs[4], u_bufs[5]), (u_bufs[6], u_bufs[7])))
    groups = [(f, min(2, nf - f)) for f in range(0, nf, 2)]

    @pl.when(t == 0)
    def _():
        cg_s[...] = jnp.zeros_like(cg_s)
        cv_s[...] = jnp.zeros_like(cv_s)

    x1 = x_ref[...]
    for j, y_ref in enumerate((ya_ref, yb_ref, yc_ref, yd_ref)):
        x1 = x1 + jnp.dot(y_ref[...], wo_ref[j * gw:(j + 1) * gw, :], preferred_element_type=F32)
    o_ref[...] = x1
    h_s[...] = _rms_rows(x1, ng_ref[...]).astype(BF16)

    def up(f, ug_s, uv_s):
        h = h_s[...]
        for w_ref, u_s, c_s in ((wg_ref, ug_s, cg_s), (wv_ref, uv_s, cv_s)):
            u_s[halo:halo + tile, :] = jnp.dot(h, w_ref[f], preferred_element_type=F32)
            u_s[0:halo, :] = c_s[f]
            c_s[f] = u_s[tile:tile + halo, :]

    def activation(f, ug_s, uv_s):
        def conv(u_s, cw_ref, cb_ref):
            cw = cw_ref[f]
            y = cb_ref[f] + cw[FFN_CONV - 1:FFN_CONV, :] * u_s[halo:halo + tile, :]
            for j in range(FFN_CONV - 1):
                y = y + cw[j:j + 1, :] * u_s[pl.ds(halo - (FFN_CONV - 1) + j, tile), :]
            return y

        return (_silu(conv(ug_s, cwg_ref, cbg_ref)) * conv(uv_s, cwv_ref, cbv_ref)).astype(BF16)

    def act_group(gi):
        f, n = groups[gi]
        for k in range(n):
            a_bufs[gi % 2][:, k * cols:(k + 1) * cols] = activation(f + k, *slots[gi % 2][k])

    def down_group(gi):
        f, n = groups[gi]
        o_ref[...] += jnp.dot(a_bufs[gi % 2][:, 0:n * cols], wd_ref[f * cols:(f + n) * cols, :],
                              preferred_element_type=F32)

    for k in range(groups[0][1]):
        up(groups[0][0] + k, *slots[0][k])
    for gi in range(len(groups)):
        if gi + 1 < len(groups):
            f2, n2 = groups[gi + 1]
            for k in range(n2):
                up(f2 + k, *slots[(gi + 1) % 2][k])
        act_group(gi)
        if gi > 0:
            down_group(gi - 1)
    down_group(len(groups) - 1)


def _ffn_call(x, ya, yb, yc, yd, wo, ng, w_up, conv_w, conv_b, w_down):
    bsz, s, d = x.shape
    tile = min(FFN_TILE, s)
    cols = FFN_COLS
    nf = FFN_DIM // cols
    assert nf * cols == FFN_DIM
    gw = GROUP_WIDTH
    halo = SUBLANES
    w_up_t = w_up.reshape(d, 2 * nf, cols).transpose(1, 0, 2)
    cw_t = conv_w.reshape(FFN_CONV, 2 * nf, cols).transpose(1, 0, 2)
    cb_t = conv_b.reshape(2 * nf, 1, cols)
    row = lambda b, t: (b, t, 0)
    const2 = lambda b, t: (0, 0)
    lo = lambda b, t: (0, 0, 0)
    hi = lambda b, t: (1, 0, 0)
    once = pl.Buffered(1)
    return pl.pallas_call(
        functools.partial(_ffn_kernel, tile=tile, nf=nf),
        grid=(bsz, s // tile),
        in_specs=[
            pl.BlockSpec((None, tile, d), row),
            pl.BlockSpec((None, tile, gw), row),
            pl.BlockSpec((None, tile, gw), row),
            pl.BlockSpec((None, tile, gw), row),
            pl.BlockSpec((None, tile, gw), row),
            pl.BlockSpec((d, d), const2, pipeline_mode=once),
            pl.BlockSpec((1, d), const2),
            pl.BlockSpec((nf, d, cols), lo, pipeline_mode=once),
            pl.BlockSpec((nf, d, cols), hi, pipeline_mode=once),
            pl.BlockSpec((nf, FFN_CONV, cols), lo),
            pl.BlockSpec((nf, FFN_CONV, cols), hi),
            pl.BlockSpec((nf, 1, cols), lo),
            pl.BlockSpec((nf, 1, cols), hi),
            pl.BlockSpec((FFN_DIM, d), const2, pipeline_mode=once),
        ],
        out_specs=pl.BlockSpec((None, tile, d), row),
        out_shape=jax.ShapeDtypeStruct((bsz, s, d), F32),
        scratch_shapes=[
            pltpu.VMEM((tile, d), BF16),
            pltpu.VMEM((nf, halo, cols), F32),
            pltpu.VMEM((nf, halo, cols), F32),
        ] + [pltpu.VMEM((tile, 2 * cols), BF16)] * 2
        + [pltpu.VMEM((tile + halo, cols), F32)] * 8,
        compiler_params=pltpu.CompilerParams(
            dimension_semantics=("parallel", "arbitrary"), vmem_limit_bytes=VMEM_LIMIT_BYTES),
    )(x, ya, yb, yc, yd, wo, ng, w_up_t, w_up_t, cw_t, cw_t, cb_t, cb_t, w_down)


def _lane_group_mask(n_groups, width, rows=SUBLANES):
    lane = np.arange(n_groups * width) // width
    m = np.zeros((rows, n_groups * width), np.float32)
    for g in range(n_groups):
        m[g] = (lane == g)
    return jnp.asarray(m)


def _block_diag_mask(n, rows_per, cols_per):
    r = np.arange(n * rows_per)[:, None] // rows_per
    c = np.arange(n * cols_per)[None, :] // cols_per
    return jnp.asarray((r == c).astype(np.float32))


def _rope_tables(s, inv_freq, n_heads):
    ang = jnp.arange(s, dtype=F32)[:, None] * inv_freq[None, :]
    cos = jnp.cos(ang)
    sin = jnp.sin(ang)
    cos_t = jnp.tile(jnp.concatenate([cos, cos], axis=-1), (1, n_heads))
    sin_t = jnp.tile(jnp.concatenate([-sin, sin], axis=-1), (1, n_heads))
    return cos_t, sin_t


def kernel(x, attn_norm_g, w_in, moba_q_norm_g, moba_k_norm_g, ssm_conv_w, ssm_conv_b, ssm_dt_bias, ssm_a_log,
           ssm_d, ssm_norm_g, gla_gate_w2, gla_gate_b, gla_norm_g, ret_norm_g, w_out, ffn_norm_g, ffn_w_up,
           ffn_conv_w, ffn_conv_b, ffn_w_down):
    bsz, s, d = x.shape
    depth = w_in.shape[0]
    gw = GROUP_WIDTH

    moba_inv = ROPE_THETA ** (-jnp.arange(0, MOBA_HEAD_DIM, 2, dtype=F32) / MOBA_HEAD_DIM)
    moba_cos, moba_sin = _rope_tables(s, moba_inv, MOBA_HEADS)
    ret_inv = 1.0 / (ROPE_THETA ** jnp.linspace(0.0, 1.0, RET_DK // 2, dtype=F32))
    ret_cos, ret_sin = _rope_tables(s, ret_inv, RET_HEADS)
    head_mean = (_block_diag_mask(4, 64, 64) / 64.0).astype(BF16)
    hmask256 = _lane_group_mask(4, 64)
    qmask128 = _lane_group_mask(4, 32)
    ssm_gmask = _lane_group_mask(SSM_GROUPS, SSM_STATE)
    ssm_smask = _block_diag_mask(SSM_GROUPS, SSM_STATE, 2 * SSM_HEAD_DIM)
    lin_smask = _block_diag_mask(4, 32, 64)
    tri128 = jnp.asarray(np.tril(np.ones((SSM_CHUNK, SSM_CHUNK), np.float32))).astype(BF16)
    P = 2 * GLA_CHUNK
    same_chunk = (np.arange(P)[:, None] // GLA_CHUNK) == (np.arange(P)[None, :] // GLA_CHUNK)
    gla_tri = jnp.asarray((np.tril(np.ones((P, P))) * same_chunk).astype(np.float32)).astype(BF16)

    L = RET_CHUNK
    ret_lg = jnp.log(1.0 - 2.0 ** (-5.0 - jnp.arange(RET_HEADS, dtype=F32)))
    idx = jnp.arange(L, dtype=F32)
    diff = idx[:, None] - idx[None, :]
    ret_dmat = jnp.where(diff >= 0, jnp.exp(jnp.maximum(diff, 0.0)[None] * ret_lg[:, None, None]), 0.0)
    ret_dmat = ret_dmat.reshape(RET_HEADS * L, L)
    ret_kend = jnp.repeat(jnp.exp((L - 1 - idx)[:, None] * ret_lg[None, :]), RET_DK, axis=1)
    ret_qdec = jnp.repeat(jnp.exp((idx + 1.0)[:, None] * ret_lg[None, :]), RET_DK, axis=1)
    ret_cdec = jnp.repeat(jnp.exp(L * ret_lg), RET_DV)[None, :]

    o_mq, o_mk, o_mv = 0, 256, 512
    o_sz, o_sx, o_sd = 768, 1024, 1536
    o_gq, o_gk, o_gv, o_gr, o_gg = 1540, 1668, 1796, 2052, 2308
    o_rq = 2324

    w_in_t = jnp.transpose(w_in, (0, 2, 1))
    for l in range(depth):
        wl = w_in_t[l]
        w_moba_qk = wl[o_mq:o_mv].astype(BF16)
        w_moba_vt = wl[o_mv:o_sz].astype(BF16)
        w_ssm = jnp.concatenate(
            [wl[o_sz:o_sd], jnp.repeat(wl[o_sd:o_sd + SSM_HEADS], SSM_HEAD_DIM, axis=0)], axis=0).astype(BF16)
        w_gla = jnp.concatenate(
            [wl[o_gq:o_gg + GLA_GATE_RANK], jnp.zeros((LANES - GLA_GATE_RANK, d), F32)], axis=0).astype(BF16)
        w_ret = wl[o_rq:o_rq + 768].astype(BF16)
        ng = attn_norm_g[l][None, :]

        y_moba = _moba_call(x, ng, w_moba_qk, w_moba_vt, jnp.tile(moba_q_norm_g[l], MOBA_HEADS)[None, :],
                            jnp.tile(moba_k_norm_g[l], MOBA_HEADS)[None, :], moba_cos, moba_sin, head_mean,
                            hmask256)
        w2 = jnp.concatenate([gla_gate_w2[l], jnp.zeros((LANES - GLA_GATE_RANK, GLA_HEADS * GLA_DK), F32)],
                             axis=0).astype(BF16)
        tile = min(MIX_TILE, s)
        y_ssm, y_gla, y_ret = _mixers_call(x, ng, [
            _ssd_spec(tile, w_ssm, ssm_conv_w[l], ssm_conv_b[l][None, :],
                      jnp.repeat(ssm_dt_bias[l], SSM_HEAD_DIM)[None, :],
                      jnp.repeat(ssm_a_log[l], SSM_HEAD_DIM)[None, :],
                      jnp.repeat(ssm_d[l], SSM_HEAD_DIM)[None, :], ssm_norm_g[l][None, :], tri128, ssm_gmask,
                      hmask256, ssm_smask),
            _gla_spec(tile, w_gla, w2, gla_gate_b[l][None, :], jnp.tile(gla_norm_g[l], GLA_HEADS)[None, :],
                      gla_tri, head_mean, qmask128, hmask256, lin_smask),
            _ret_spec(tile, w_ret, ret_cos, ret_sin, jnp.tile(ret_norm_g[l], RET_HEADS)[None, :], ret_dmat,
                      ret_kend, ret_qdec, ret_cdec, head_mean, qmask128, hmask256, lin_smask),
        ], REC_ROWS)
        x = _ffn_call(x, y_moba, y_ssm, y_gla, y_ret, w_out[l].astype(BF16), ffn_norm_g[l][None, :],
                      ffn_w_up[l].astype(BF16), ffn_conv_w[l], ffn_conv_b[l][None, :], ffn_w_down[l].astype(BF16))
    return x
```

```python
import functools

import numpy as np
import jax
import jax.numpy as jnp
from jax import lax
from jax.experimental import pallas as pl
from jax.experimental.pallas import tpu as pltpu

F32 = jnp.float32
BF16 = jnp.bfloat16

D_MODEL = 1024
GROUP_WIDTH = 256
NORM_EPS = 1e-6
NEG_INF = -1e30

MOBA_HEADS = 4
MOBA_HEAD_DIM = 64
MOBA_BLOCK = 256
MOBA_TOPK = 3
MOBA_ONES_ROWS = 16
ROPE_THETA = 10000.0
LOG2_E = 1.4426950408889634

SSM_HEADS = 4
SSM_HEAD_DIM = 64
SSM_GROUPS = 2
SSM_STATE = 64
SSM_CONV = 4
SSM_CHUNK = 128
SSM_CONV_CH = 512

GLA_HEADS = 4
GLA_DV = 64
GLA_DK = 32
GLA_GATE_RANK = 16
GLA_GATE_TAU = 16.0
GLA_CHUNK = 64

RET_HEADS = 4
RET_DV = 64
RET_DK = 32
RET_CHUNK = 128

FFN_DIM = 2816
FFN_CONV = 3

LANES = 128
SUBLANES = 8
VMEM_LIMIT_BYTES = 48 * 1024 * 1024

MIX_ROWS = 2
REC_ROWS = 1
MIX_TILE = 512
FFN_TILE = 512
FFN_COLS = 256


def _rms_rows(x, g):
    ms = jnp.mean(x * x, axis=-1, keepdims=True)
    return x * lax.rsqrt(ms + NORM_EPS) * g


def _dot(a, b):
    return jnp.dot(a.astype(BF16), b.astype(BF16), preferred_element_type=F32)


def _dot_nt(a, b):
    return lax.dot_general(a.astype(BF16), b.astype(BF16), (((1,), (1,)), ((), ())),
                           preferred_element_type=F32)


def _silu(x):
    half = 0.5 * x
    return half + half * jnp.tanh(half)


def _softplus(x):
    return jnp.maximum(x, 0.0) + jnp.log(1.0 + jnp.exp(-jnp.abs(x)))


def _split_dot(a, b, passes, split_rhs=False):
    acc = None
    rest = b if split_rhs else a
    for j in range(passes):
        piece = rest.astype(BF16)
        part = jnp.dot(a if split_rhs else piece, piece if split_rhs else b, preferred_element_type=F32)
        acc = part if acc is None else acc + part
        if j + 1 < passes:
            rest = rest - piece.astype(F32)
    return acc


def _round_robin(generators):
    done = object()
    running = list(generators)
    while running:
        running = [gen for gen in running if next(gen, done) is not done]


def _rotate_half(t, half):
    slabs = []
    for c in range(t.shape[-1] // LANES):
        u = t[:, c * LANES:(c + 1) * LANES]
        lane = lax.broadcasted_iota(jnp.int32, u.shape, 1)
        first = (lane % (2 * half)) < half
        slabs.append(jnp.where(first, pltpu.roll(u, LANES - half, 1), pltpu.roll(u, half, 1)))
    return slabs[0] if len(slabs) == 1 else jnp.concatenate(slabs, axis=-1)


def _moba_prep(i, x_ref, g_ref, wqk_ref, wvt_ref, qg_ref, kg_ref, cos_ref, sin_ref, gm_ref, hmask_ref,
               k_s, vt_s, km_s, qm_s, bias_s, *, n_blk):
    blk = MOBA_BLOCK
    gw = GROUP_WIDTH
    dh = MOBA_HEAD_DIM

    h = _rms_rows(x_ref[...], g_ref[...]).astype(BF16)
    p = _dot_nt(h, wqk_ref[...])
    vt = _dot_nt(wvt_ref[...], h)
    gm = gm_ref[...]
    cos = cos_ref[...]
    sin = sin_ref[...]

    def prep(t, gain):
        ms = _split_dot(t * t, gm, 2)
        t = t * lax.rsqrt(ms + NORM_EPS) * gain
        return t * cos + _rotate_half(t, dh // 2) * sin

    yield
    q = prep(p[:, 0:gw], qg_ref[...])
    k = prep(p[:, gw:2 * gw], kg_ref[...])
    yield

    k_s[pl.ds(pl.multiple_of(i * blk, blk), blk), :] = k.astype(BF16)
    vtb = vt.astype(BF16)
    for hd in range(MOBA_HEADS):
        vt_s[i, hd, 0:dh, :] = vtb[hd * dh:(hd + 1) * dh, :]
        vt_s[i, hd, dh:dh + MOBA_ONES_ROWS, :] = jnp.ones((MOBA_ONES_ROWS, blk), BF16)
    km_s[i] = jnp.broadcast_to(jnp.mean(k, axis=0, keepdims=True), (SUBLANES, gw))

    row8 = lax.broadcasted_iota(jnp.int32, (SUBLANES, gw), 0)
    kmat = jnp.zeros((SUBLANES, gw), F32)
    for n in range(n_blk):
        kmat = jnp.where(row8 == n, km_s[n], kmat)
    hmask = hmask_ref[...]
    kmh = jnp.concatenate([kmat * hmask[hd:hd + 1, :] for hd in range(MOBA_HEADS)], axis=0)
    km_hi = kmh.astype(BF16)
    km_lo = (kmh - km_hi.astype(F32)).astype(BF16)
    q_hi = q.astype(BF16)
    q_lo = (q - q_hi.astype(F32)).astype(BF16)
    gate = _dot_nt(km_hi, q_hi) + _dot_nt(km_hi, q_lo) + _dot_nt(km_lo, q_hi)
    scale = dh ** -0.5 * LOG2_E
    for hd in range(MOBA_HEADS):
        qm_s[hd] = (q * (scale * hmask[hd:hd + 1, :])).astype(BF16)
    yield

    rowq = lax.broadcasted_iota(jnp.int32, (SUBLANES, blk), 0)
    valid = rowq < i
    bias_tiles = [jnp.zeros((SUBLANES, blk), F32) for _ in range(n_blk)]
    for hd in range(MOBA_HEADS):
        gh = gate[hd * SUBLANES:(hd + 1) * SUBLANES, :]
        for n in range(n_blk):
            gn = gh[n:n + 1, :]
            beats = valid & ((gh > gn) | ((gh == gn) & (rowq < n)))
            cnt = jnp.sum(beats.astype(F32), axis=0, keepdims=True)
            bias_n = jnp.where(cnt < float(MOBA_TOPK), 0.0, NEG_INF)
            bias_tiles[n] = jnp.where(rowq == hd, bias_n, bias_tiles[n])
    for n in range(n_blk):
        bias_s[n] = bias_tiles[n]


def _moba_attend(ii, y_ref, k_s, vt_s, qm_s, bias_s, s_s):
    blk = MOBA_BLOCK
    dh = MOBA_HEAD_DIM
    rj = lax.broadcasted_iota(jnp.int32, (blk, blk), 0)
    cq = lax.broadcasted_iota(jnp.int32, (blk, blk), 1)
    causal = rj <= cq
    outs = []
    for hd in range(MOBA_HEADS):
        qh = qm_s[hd]
        m = None
        for n in range(ii + 1):
            s = _dot_nt(k_s[n * blk:(n + 1) * blk, :], qh)
            if n == ii:
                s = jnp.where(causal, s, NEG_INF)
            s_s[hd, n * blk:(n + 1) * blk, :] = s
            mx = jnp.max(s, axis=0, keepdims=True)
            if n < ii:
                mx = mx + bias_s[n, hd:hd + 1, :]
            m = mx if m is None else jnp.maximum(m, mx)
        acc = jnp.zeros((dh + MOBA_ONES_ROWS, blk), F32)
        for n in range(ii + 1):
            shift = m if n == ii else m - bias_s[n, hd:hd + 1, :]
            pr = jnp.exp2(s_s[hd, n * blk:(n + 1) * blk, :] - shift)
            acc = acc + jnp.dot(vt_s[n, hd], pr.astype(BF16), preferred_element_type=F32)
        outs.append(acc[0:dh, :] * (1.0 / acc[dh:dh + 1, :]))
        yield
    y_ref[...] = jnp.concatenate(outs, axis=0).T.astype(y_ref.dtype)


def _moba_kernel(*refs, n_blk, rows):
    i = pl.program_id(1)
    x_ref, shared, y_ref = refs[0], refs[1:10], refs[10]
    k_s, vt_s, km_s, qm_s, bias_s, s_s = refs[11:]

    @pl.when(i == 0)
    def _():
        km_s[...] = jnp.zeros_like(km_s)

    _round_robin([_moba_prep(i, x_ref.at[r], *shared, k_s.at[r], vt_s.at[r], km_s.at[r], qm_s.at[r],
                             bias_s.at[r], n_blk=n_blk) for r in range(rows)])

    def attend(ii):
        _round_robin([_moba_attend(ii, y_ref.at[r], k_s.at[r], vt_s.at[r], qm_s.at[r], bias_s.at[r], s_s.at[r])
                      for r in range(rows)])

    for ii in range(n_blk):
        pl.when(i == ii)(functools.partial(attend, ii))


def _moba_call(x, norm_g, wqk, wvt, qg, kg, cos_t, sin_t, gm, hmask):
    bsz, s, d = x.shape
    blk = MOBA_BLOCK
    n_blk = s // blk
    gw = GROUP_WIDTH
    rows = MIX_ROWS
    assert bsz % rows == 0
    const = lambda b, i: (0, 0)
    return pl.pallas_call(
        functools.partial(_moba_kernel, n_blk=n_blk, rows=rows),
        grid=(bsz // rows, n_blk),
        in_specs=[
            pl.BlockSpec((rows, blk, d), lambda b, i: (b, i, 0)),
            pl.BlockSpec((1, d), const),
            pl.BlockSpec((2 * gw, d), const),
            pl.BlockSpec((gw, d), const),
            pl.BlockSpec((1, gw), const),
            pl.BlockSpec((1, gw), const),
            pl.BlockSpec((blk, gw), lambda b, i: (i, 0)),
            pl.BlockSpec((blk, gw), lambda b, i: (i, 0)),
            pl.BlockSpec((gw, gw), const),
            pl.BlockSpec((SUBLANES, gw), const),
        ],
        out_specs=pl.BlockSpec((rows, blk, gw), lambda b, i: (b, i, 0)),
        out_shape=jax.ShapeDtypeStruct((bsz, s, gw), BF16),
        scratch_shapes=[
            pltpu.VMEM((rows, s, gw), BF16),
            pltpu.VMEM((rows, n_blk, MOBA_HEADS, MOBA_HEAD_DIM + MOBA_ONES_ROWS, blk), BF16),
            pltpu.VMEM((rows, n_blk, SUBLANES, gw), F32),
            pltpu.VMEM((rows, MOBA_HEADS, blk, gw), BF16),
            pltpu.VMEM((rows, n_blk, SUBLANES, blk), F32),
            pltpu.VMEM((rows, MOBA_HEADS, s, blk), F32),
        ],
        compiler_params=pltpu.CompilerParams(
            dimension_semantics=("parallel", "arbitrary"), vmem_limit_bytes=VMEM_LIMIT_BYTES),
    )(x, norm_g, wqk, wvt, qg, kg, cos_t, sin_t, gm, hmask)


def _whole(a):
    return pl.BlockSpec(a.shape, lambda b, t: (0,) * a.ndim)


def _mixers_kernel(streams, rows):
    def kernel(*refs):
        x_ref, g_ref = refs[0], refs[1]
        pos = 2
        shared = []
        for _, n_shared, _ in streams:
            shared.append(refs[pos:pos + n_shared])
            pos += n_shared
        outs = refs[pos:pos + len(streams)]
        pos += len(streams)
        scratch = []
        for _, _, n_scratch in streams:
            scratch.append(refs[pos:pos + n_scratch])
            pos += n_scratch

        @pl.when(pl.program_id(1) == 0)
        def _():
            for group in scratch:
                for s_ref in group:
                    s_ref[...] = jnp.zeros_like(s_ref)

        running = []
        for r in range(rows):
            h = _rms_rows(x_ref[r], g_ref[...]).astype(BF16)
            for (row_fn, _, _), sh, y_ref, sc in zip(streams, shared, outs, scratch):
                running.append(row_fn(h, *sh, y_ref.at[r], *(s_ref.at[r] for s_ref in sc)))
        _round_robin(running)

    return kernel


def _mixers_call(x, norm_g, specs, rows):
    bsz, s, d = x.shape
    tile = min(MIX_TILE, s)
    gw = GROUP_WIDTH
    assert bsz % rows == 0
    row_map = lambda b, t: (b, t, 0)
    in_specs = [pl.BlockSpec((rows, tile, d), row_map), pl.BlockSpec((1, d), lambda b, t: (0, 0))]
    arrays = [x, norm_g]
    scratch = []
    for _, arrs, blockspecs, scr in specs:
        arrays += list(arrs)
        in_specs += list(blockspecs)
        scratch += [pltpu.VMEM((rows,) + shape, dtype) for shape, dtype in scr]
    return pl.pallas_call(
        _mixers_kernel([(fn, len(arrs), len(scr)) for fn, arrs, _, scr in specs], rows),
        grid=(bsz // rows, s // tile),
        in_specs=in_specs,
        out_specs=[pl.BlockSpec((rows, tile, gw), row_map)] * len(specs),
        out_shape=[jax.ShapeDtypeStruct((bsz, s, gw), BF16)] * len(specs),
        scratch_shapes=scratch,
        compiler_params=pltpu.CompilerParams(
            dimension_semantics=("parallel", "arbitrary"), vmem_limit_bytes=VMEM_LIMIT_BYTES),
    )(*arrays)


def _ssd_row(h, w_ref, cw_ref, cb_ref, dtb_ref, alog_ref, dsk_ref, ng_ref, tri_ref,
             gmask_ref, hmask_ref, smask_ref, y_ref, xp_s, st_s, y_s, *, tile):
    gw = GROUP_WIDTH
    L = SSM_CHUNK
    gn = SSM_GROUPS * SSM_STATE

    p = _dot_nt(h, w_ref[...])
    z = p[:, 0:gw]
    xp_s[SUBLANES:SUBLANES + tile, :] = p[:, gw:gw + SSM_CONV_CH]
    cw = cw_ref[...]
    conv = cb_ref[...]
    for j in range(SSM_CONV):
        conv = conv + cw[j:j + 1, :] * xp_s[pl.ds(SUBLANES - (SSM_CONV - 1) + j, tile), :]
    xp_s[0:SUBLANES, :] = xp_s[tile:tile + SUBLANES, :]
    xbc = _silu(conv)
    xs = xbc[:, 0:gw]
    bm = xbc[:, gw:gw + gn]
    cm = xbc[:, gw + gn:gw + 2 * gn]
    dt = _softplus(p[:, gw + SSM_CONV_CH:2 * gw + SSM_CONV_CH] + dtb_ref[...])
    a = dt * (-jnp.exp(alog_ref[...]))
    xdt = xs * dt

    tri = tri_ref[...]
    gmask = gmask_ref[...]
    hmask = hmask_ref[...]
    smask = smask_ref[...]
    rl = lax.broadcasted_iota(jnp.int32, (L, L), 0)
    cs = lax.broadcasted_iota(jnp.int32, (L, L), 1)
    causal = rl >= cs
    rep = SSM_HEADS // SSM_GROUPS
    yield
    for c in range(tile // L):
        sl = slice(c * L, (c + 1) * L)
        a_cs = _split_dot(tri, a[sl], 3, split_rhs=True)
        a_cs_t = a_cs.T
        a_last = a_cs[L - 1:L, :]
        cmc = cm[sl]
        bmc = bm[sl]
        xdtc = xdt[sl]
        scores = _dot_nt(jnp.concatenate([cmc * gmask[g:g + 1, :] for g in range(SSM_GROUPS)], axis=0), bmc)
        weighted = []
        for hd in range(SSM_HEADS):
            g = hd // rep
            col = a_cs[:, hd * SSM_HEAD_DIM:hd * SSM_HEAD_DIM + 1]
            row = a_cs_t[hd * SSM_HEAD_DIM:hd * SSM_HEAD_DIM + 1, :]
            decay = jnp.where(causal, jnp.exp(col - row), 0.0)
            weighted.append(scores[g * L:(g + 1) * L, :] * decay)
        y_heads = _dot(jnp.concatenate(weighted, axis=0), xdtc)
        state = st_s[...]
        y = _dot(cmc, state) * jnp.exp(a_cs)
        for hd in range(SSM_HEADS):
            y = y + y_heads[hd * L:(hd + 1) * L, :] * hmask[hd:hd + 1, :]
        new = _dot(bmc.T, xdtc * jnp.exp(a_last - a_cs)) * smask
        st_s[...] = state * jnp.exp(a_last) + new
        y_s[sl, :] = y
        yield

    y = (y_s[...] + xs * dsk_ref[...]) * _silu(z)
    ng = ng_ref[...]
    half = gw // SSM_GROUPS
    for g in range(SSM_GROUPS):
        yg = y[:, g * half:(g + 1) * half]
        ms = jnp.mean(yg * yg, axis=-1, keepdims=True)
        y_ref[:, g * half:(g + 1) * half] = (yg * lax.rsqrt(ms + NORM_EPS) * ng[:, g * half:(g + 1) * half]
                                             ).astype(y_ref.dtype)


def _ssd_spec(tile, w, cw, cb, dtb, alog, dsk, ng, tri, gmask, hmask, smask):
    gw = GROUP_WIDTH
    gn = SSM_GROUPS * SSM_STATE
    arrays = (w, cw, cb, dtb, alog, dsk, ng, tri, gmask, hmask, smask)
    scratch = [((tile + SUBLANES, SSM_CONV_CH), F32),
               ((gn, gw), F32),
               ((tile, gw), F32)]
    return functools.partial(_ssd_row, tile=tile), arrays, [_whole(a) for a in arrays], scratch


def _gla_row(h, w_ref, w2_ref, b2_ref, ng_ref, tri_ref, gm_ref, qmask_ref,
             vmask_ref, smask_ref, y_ref, st_s, o_s, *, tile):
    gw = GROUP_WIDTH
    L = GLA_CHUNK
    P = 2 * L
    qk = GLA_HEADS * GLA_DK

    p = _dot_nt(h, w_ref[...])
    q = p[:, 0:qk] * (GLA_DK ** -0.5)
    k = p[:, qk:2 * qk]
    v = p[:, 2 * qk:2 * qk + gw]
    r = p[:, 2 * qk + gw:2 * qk + 2 * gw]
    g_pre = _dot(p[:, 2 * qk + 2 * gw:2 * qk + 2 * gw + LANES], w2_ref[...]) + b2_ref[...]
    lg = -_softplus(-g_pre) / GLA_GATE_TAU

    tri = tri_ref[...]
    qmask = qmask_ref[...]
    vmask = vmask_ref[...]
    smask = smask_ref[...]
    rl = lax.broadcasted_iota(jnp.int32, (P, P), 0)
    cs = lax.broadcasted_iota(jnp.int32, (P, P), 1)
    causal = (rl >= cs) & ((rl // L) == (cs // L))
    lane_p = lax.broadcasted_iota(jnp.int32, (qk, P), 1)
    row_p = lax.broadcasted_iota(jnp.int32, (P, qk), 0)
    yield
    for c in range(tile // P):
        sl = slice(c * P, (c + 1) * P)
        gc = lg[sl]
        bcs = _split_dot(tri, gc, 3, split_rhs=True)
        b_last = jnp.where(row_p < L, bcs[L - 1:L, :], bcs[P - 1:P, :])
        qd = q[sl] * jnp.exp(bcs)
        ki = k[sl] * jnp.exp(-bcs)
        ke_t = (k[sl] * jnp.exp(b_last - bcs)).T
        dec_t = jnp.exp(b_last).T
        vc = v[sl]
        o = jnp.zeros((P, gw), F32)
        for hd in range(GLA_HEADS):
            att = jnp.where(causal, _dot_nt(qd * qmask[hd:hd + 1, :], ki), 0.0)
            o = o + _dot(att, vc * vmask[hd:hd + 1, :])
        o_s[sl, :] = o
        for j in range(2):
            rows = slice(c * P + j * L, c * P + (j + 1) * L)
            state = st_s[...]
            o_s[rows, :] = o_s[rows, :] + _dot(qd[j * L:(j + 1) * L], state)
            in_chunk = (lane_p // L) == j
            new = _dot(jnp.where(in_chunk, ke_t, 0.0), vc) * smask
            st_s[...] = state * dec_t[:, j * L:j * L + 1] + new
        yield

    o = o_s[...]
    ms = _split_dot(o * o, gm_ref[...], 2)
    y_ref[...] = (o * lax.rsqrt(ms + NORM_EPS) * ng_ref[...] * _silu(r)).astype(y_ref.dtype)


def _gla_spec(tile, w, w2, b2, ng, tri, gm, qmask, vmask, smask):
    arrays = (w, w2, b2, ng, tri, gm, qmask, vmask, smask)
    scratch = [((GLA_HEADS * GLA_DK, GROUP_WIDTH), F32),
               ((tile, GROUP_WIDTH), F32)]
    return functools.partial(_gla_row, tile=tile), arrays, [_whole(a) for a in arrays], scratch


def _ret_row(h, w_ref, cos_ref, sin_ref, ng_ref, dmat_ref, kend_ref, qdec_ref, cdec_ref,
             gm_ref, qmask_ref, vmask_ref, smask_ref, y_ref, st_s, o_s, *, tile):
    gw = GROUP_WIDTH
    L = RET_CHUNK
    qk = RET_HEADS * RET_DK

    p = _dot_nt(h, w_ref[...])
    cos = cos_ref[...]
    sin = sin_ref[...]

    def rope(u):
        return u * cos + _rotate_half(u, RET_DK // 2) * sin

    q = rope(p[:, 0:qk])
    k = rope(p[:, qk:2 * qk]) * (RET_DK ** -0.5)
    v = p[:, 2 * qk:2 * qk + gw]
    gate = p[:, 2 * qk + gw:2 * qk + 2 * gw]

    dmat = dmat_ref[...]
    kend = kend_ref[...]
    qdec = qdec_ref[...]
    cdec = cdec_ref[...]
    qmask = qmask_ref[...]
    vmask = vmask_ref[...]
    smask = smask_ref[...]
    yield
    for c in range(tile // L):
        sl = slice(c * L, (c + 1) * L)
        qc = q[sl]
        kc = k[sl]
        vc = v[sl]
        q_heads = jnp.concatenate([qc * qmask[hd:hd + 1, :] for hd in range(RET_HEADS)], axis=0)
        o_heads = _dot(_dot_nt(q_heads, kc) * dmat, vc)
        state = st_s[...]
        o = _dot(qc * qdec, state)
        for hd in range(RET_HEADS):
            o = o + o_heads[hd * L:(hd + 1) * L, :] * vmask[hd:hd + 1, :]
        new = _dot((kc * kend).T, vc) * smask
        st_s[...] = state * cdec + new
        o_s[sl, :] = o
        yield

    o = o_s[...]
    ms = _split_dot(o * o, gm_ref[...], 2)
    y_ref[...] = (o * lax.rsqrt(ms + NORM_EPS) * ng_ref[...] * _silu(gate)).astype(y_ref.dtype)


def _ret_spec(tile, w, cos_t, sin_t, ng, dmat, kend, qdec, cdec, gm, qmask, vmask, smask):
    qk = RET_HEADS * RET_DK
    arrays = (w, cos_t, sin_t, ng, dmat, kend, qdec, cdec, gm, qmask, vmask, smask)
    by_tile = pl.BlockSpec((tile, qk), lambda b, t: (t, 0))
    specs = [_whole(w), by_tile, by_tile] + [_whole(a) for a in arrays[3:]]
    scratch = [((qk, GROUP_WIDTH), F32),
               ((tile, GROUP_WIDTH), F32)]
    return functools.partial(_ret_row, tile=tile), arrays, specs, scratch


def _ffn_kernel(x_ref, ya_ref, yb_ref, yc_ref, yd_ref, wo_ref, ng_ref, wg_ref, wv_ref, cwg_ref, cwv_ref,
                cbg_ref, cbv_ref, wd_ref, o_ref, h_s, cg_s, cv_s, *bufs, tile, nf):
    t = pl.program_id(1)
    gw = GROUP_WIDTH
    cols = FFN_COLS
    halo = SUBLANES
    a_bufs, u_bufs = bufs[0:2], bufs[2:]
    slots = (((u_bufs[0], u_bufs[1]), (u_bufs[2], u_bufs[3])), ((u_bufs[4], u_bufs[5]), (u_bufs[6], u_bufs[7])))
    groups = [(f, min(2, nf - f)) for f in range(0, nf, 2)]

    @pl.when(t == 0)
    def _():
        cg_s[...] = jnp.zeros_like(cg_s)
        cv_s[...] = jnp.zeros_like(cv_s)

    x1 = x_ref[...]
    for j, y_ref in enumerate((ya_ref, yb_ref, yc_ref, yd_ref)):
        x1 = x1 + jnp.dot(y_ref[...], wo_ref[j * gw:(j + 1) * gw, :], preferred_element_type=F32)
    o_ref[...] = x1
    h_s[...] = _rms_rows(x1, ng_ref[...]).astype(BF16)

    def up(f, ug_s, uv_s):
        h = h_s[...]
        for w_ref, u_s, c_s in ((wg_ref, ug_s, cg_s), (wv_ref, uv_s, cv_s)):
            u_s[halo:halo + tile, :] = jnp.dot(h, w_ref[:, f * cols:(f + 1) * cols], preferred_element_type=F32)
            u_s[0:halo, :] = c_s[f]
            c_s[f] = u_s[tile:tile + halo, :]

    def activation(f, ug_s, uv_s):
        def conv(u_s, cw_ref, cb_ref):
            cw = cw_ref[f]
            y = cb_ref[f] + cw[FFN_CONV - 1:FFN_CONV, :] * u_s[halo:halo + tile, :]
            for j in range(FFN_CONV - 1):
                y = y + cw[j:j + 1, :] * u_s[pl.ds(halo - (FFN_CONV - 1) + j, tile), :]
            return y

        return (_silu(conv(ug_s, cwg_ref, cbg_ref)) * conv(uv_s, cwv_ref, cbv_ref)).astype(BF16)

    def act_group(gi):
        f, n = groups[gi]
        for k in range(n):
            a_bufs[gi % 2][:, k * cols:(k + 1) * cols] = activation(f + k, *slots[gi % 2][k])

    def down_group(gi):
        f, n = groups[gi]
        o_ref[...] += jnp.dot(a_bufs[gi % 2][:, 0:n * cols], wd_ref[f * cols:(f + n) * cols, :],
                              preferred_element_type=F32)

    for k in range(groups[0][1]):
        up(groups[0][0] + k, *slots[0][k])
    for gi in range(len(groups)):
        if gi + 1 < len(groups):
            f2, n2 = groups[gi + 1]
            for k in range(n2):
                up(f2 + k, *slots[(gi + 1) % 2][k])
        act_group(gi)
        if gi > 0:
            down_group(gi - 1)
    down_group(len(groups) - 1)


def _ffn_call(x, ya, yb, yc, yd, wo, ng, w_up, conv_w, conv_b, w_down):
    bsz, s, d = x.shape
    tile = min(FFN_TILE, s)
    cols = FFN_COLS
    nf = FFN_DIM // cols
    assert nf * cols == FFN_DIM
    gw = GROUP_WIDTH
    halo = SUBLANES
    cw_t = conv_w.reshape(FFN_CONV, 2 * nf, cols).transpose(1, 0, 2)
    cb_t = conv_b.reshape(2 * nf, 1, cols)
    row = lambda b, t: (b, t, 0)
    const2 = lambda b, t: (0, 0)
    lo = lambda b, t: (0, 0, 0)
    hi = lambda b, t: (1, 0, 0)
    once = pl.Buffered(1)
    return pl.pallas_call(
        functools.partial(_ffn_kernel, tile=tile, nf=nf),
        grid=(bsz, s // tile),
        in_specs=[
            pl.BlockSpec((None, tile, d), row),
            pl.BlockSpec((None, tile, gw), row),
            pl.BlockSpec((None, tile, gw), row),
            pl.BlockSpec((None, tile, gw), row),
            pl.BlockSpec((None, tile, gw), row),
            pl.BlockSpec((d, d), const2, pipeline_mode=once),
            pl.BlockSpec((1, d), const2),
            pl.BlockSpec((d, FFN_DIM), lambda b, t: (0, 0), pipeline_mode=once),
            pl.BlockSpec((d, FFN_DIM), lambda b, t: (0, 1), pipeline_mode=once),
            pl.BlockSpec((nf, FFN_CONV, cols), lo),
            pl.BlockSpec((nf, FFN_CONV, cols), hi),
            pl.BlockSpec((nf, 1, cols), lo),
            pl.BlockSpec((nf, 1, cols), hi),
            pl.BlockSpec((FFN_DIM, d), const2, pipeline_mode=once),
        ],
        out_specs=pl.BlockSpec((None, tile, d), row),
        out_shape=jax.ShapeDtypeStruct((bsz, s, d), F32),
        scratch_shapes=[
            pltpu.VMEM((tile, d), BF16),
            pltpu.VMEM((nf, halo, cols), F32),
            pltpu.VMEM((nf, halo, cols), F32),
        ] + [pltpu.VMEM((tile, 2 * cols), BF16)] * 2
        + [pltpu.VMEM((tile + halo, cols), F32)] * 8,
        compiler_params=pltpu.CompilerParams(
            dimension_semantics=("parallel", "arbitrary"), vmem_limit_bytes=VMEM_LIMIT_BYTES),
    )(x, ya, yb, yc, yd, wo, ng, w_up, w_up, cw_t, cw_t, cb_t, cb_t, w_down)


def _lane_group_mask(n_groups, width, rows=SUBLANES):
    lane = np.arange(n_groups * width) // width
    m = np.zeros((rows, n_groups * width), np.float32)
    for g in range(n_groups):
        m[g] = (lane == g)
    return jnp.asarray(m)


def _block_diag_mask(n, rows_per, cols_per):
    r = np.arange(n * rows_per)[:, None] // rows_per
    c = np.arange(n * cols_per)[None, :] // cols_per
    return jnp.asarray((r == c).astype(np.float32))


def _rope_tables(s, inv_freq, n_heads):
    ang = jnp.arange(s, dtype=F32)[:, None] * inv_freq[None, :]
    cos = jnp.cos(ang)
    sin = jnp.sin(ang)
    cos_t = jnp.tile(jnp.concatenate([cos, cos], axis=-1), (1, n_heads))
    sin_t = jnp.tile(jnp.concatenate([-sin, sin], axis=-1), (1, n_heads))
    return cos_t, sin_t


def kernel(x, attn_norm_g, w_in, moba_q_norm_g, moba_k_norm_g, ssm_conv_w, ssm_conv_b, ssm_dt_bias, ssm_a_log,
           ssm_d, ssm_norm_g, gla_gate_w2, gla_gate_b, gla_norm_g, ret_norm_g, w_out, ffn_norm_g, ffn_w_up,
           ffn_conv_w, ffn_conv_b, ffn_w_down):
    bsz, s, d = x.shape
    depth = w_in.shape[0]
    gw = GROUP_WIDTH

    moba_inv = ROPE_THETA ** (-jnp.arange(0, MOBA_HEAD_DIM, 2, dtype=F32) / MOBA_HEAD_DIM)
    moba_cos, moba_sin = _rope_tables(s, moba_inv, MOBA_HEADS)
    ret_inv = 1.0 / (ROPE_THETA ** jnp.linspace(0.0, 1.0, RET_DK // 2, dtype=F32))
    ret_cos, ret_sin = _rope_tables(s, ret_inv, RET_HEADS)
    head_mean = (_block_diag_mask(4, 64, 64) / 64.0).astype(BF16)
    hmask256 = _lane_group_mask(4, 64)
    qmask128 = _lane_group_mask(4, 32)
    ssm_gmask = _lane_group_mask(SSM_GROUPS, SSM_STATE)
    ssm_smask = _block_diag_mask(SSM_GROUPS, SSM_STATE, 2 * SSM_HEAD_DIM)
    lin_smask = _block_diag_mask(4, 32, 64)
    tri128 = jnp.asarray(np.tril(np.ones((SSM_CHUNK, SSM_CHUNK), np.float32))).astype(BF16)
    P = 2 * GLA_CHUNK
    same_chunk = (np.arange(P)[:, None] // GLA_CHUNK) == (np.arange(P)[None, :] // GLA_CHUNK)
    gla_tri = jnp.asarray((np.tril(np.ones((P, P))) * same_chunk).astype(np.float32)).astype(BF16)

    L = RET_CHUNK
    ret_lg = jnp.log(1.0 - 2.0 ** (-5.0 - jnp.arange(RET_HEADS, dtype=F32)))
    idx = jnp.arange(L, dtype=F32)
    diff = idx[:, None] - idx[None, :]
    ret_dmat = jnp.where(diff >= 0, jnp.exp(jnp.maximum(diff, 0.0)[None] * ret_lg[:, None, None]), 0.0)
    ret_dmat = ret_dmat.reshape(RET_HEADS * L, L)
    ret_kend = jnp.repeat(jnp.exp((L - 1 - idx)[:, None] * ret_lg[None, :]), RET_DK, axis=1)
    ret_qdec = jnp.repeat(jnp.exp((idx + 1.0)[:, None] * ret_lg[None, :]), RET_DK, axis=1)
    ret_cdec = jnp.repeat(jnp.exp(L * ret_lg), RET_DV)[None, :]

    o_mq, o_mk, o_mv = 0, 256, 512
    o_sz, o_sx, o_sd = 768, 1024, 1536
    o_gq, o_gk, o_gv, o_gr, o_gg = 1540, 1668, 1796, 2052, 2308
    o_rq = 2324

    w_in_t = jnp.transpose(w_in, (0, 2, 1))
    for l in range(depth):
        wl = w_in_t[l]
        w_moba_qk = wl[o_mq:o_mv].astype(BF16)
        w_moba_vt = wl[o_mv:o_sz].astype(BF16)
        w_ssm = jnp.concatenate(
            [wl[o_sz:o_sd], jnp.repeat(wl[o_sd:o_sd + SSM_HEADS], SSM_HEAD_DIM, axis=0)], axis=0).astype(BF16)
        w_gla = jnp.concatenate(
            [wl[o_gq:o_gg + GLA_GATE_RANK], jnp.zeros((LANES - GLA_GATE_RANK, d), F32)], axis=0).astype(BF16)
        w_ret = wl[o_rq:o_rq + 768].astype(BF16)
        ng = attn_norm_g[l][None, :]

        y_moba = _moba_call(x, ng, w_moba_qk, w_moba_vt, jnp.tile(moba_q_norm_g[l], MOBA_HEADS)[None, :],
                            jnp.tile(moba_k_norm_g[l], MOBA_HEADS)[None, :], moba_cos, moba_sin, head_mean,
                            hmask256)
        w2 = jnp.concatenate([gla_gate_w2[l], jnp.zeros((LANES - GLA_GATE_RANK, GLA_HEADS * GLA_DK), F32)],
                             axis=0).astype(BF16)
        tile = min(MIX_TILE, s)
        y_ssm, y_gla, y_ret = _mixers_call(x, ng, [
            _ssd_spec(tile, w_ssm, ssm_conv_w[l], ssm_conv_b[l][None, :],
                      jnp.repeat(ssm_dt_bias[l], SSM_HEAD_DIM)[None, :],
                      jnp.repeat(ssm_a_log[l], SSM_HEAD_DIM)[None, :],
                      jnp.repeat(ssm_d[l], SSM_HEAD_DIM)[None, :], ssm_norm_g[l][None, :], tri128, ssm_gmask,
                      hmask256, ssm_smask),
            _gla_spec(tile, w_gla, w2, gla_gate_b[l][None, :], jnp.tile(gla_norm_g[l], GLA_HEADS)[None, :],
                      gla_tri, head_mean, qmask128, hmask256, lin_smask),
            _ret_spec(tile, w_ret, ret_cos, ret_sin, jnp.tile(ret_norm_g[l], RET_HEADS)[None, :], ret_dmat,
                      ret_kend, ret_qdec, ret_cdec, head_mean, qmask128, hmask256, lin_smask),
        ], REC_ROWS)
        x = _ffn_call(x, y_moba, y_ssm, y_gla, y_ret, w_out[l].astype(BF16), ffn_norm_g[l][None, :],
                      ffn_w_up[l].astype(BF16), ffn_conv_w[l], ffn_conv_b[l][None, :], ffn_w_down[l].astype(BF16))
    return x
```
